```python
import jax, jax.numpy as jnp
from jax import lax
import numpy as np

D_MODEL = 1024
BATCH = 4
SEQ = 8192
DEPTH = 1
DEC_BATCH = 16
DEC_SEQ = 4096
PAST_LEN = 128

D_MIX = D_MODEL
D_ATTN = D_MIX // 2
D_GMLP = D_MIX - D_ATTN
HEAD_DIM = 64
N_Q_HEADS = D_ATTN // HEAD_DIM
N_KV_HEADS = 2
Q_PER_KV = N_Q_HEADS // N_KV_HEADS
ROT_DIM = HEAD_DIM // 4
ROPE_THETA = 500000.0
WINDOW = 128
BLOCK = 128
N_GMLP_GROUPS = 8
GMLP_GROUP_DIM = D_GMLP // N_GMLP_GROUPS
CHUNK = 128
D_FF = 2816
CONV_W = 3
EPS = 1e-6
D_Q = N_Q_HEADS * HEAD_DIM
D_KV = N_KV_HEADS * HEAD_DIM
D_IN_PROJ = D_Q + 2 * D_KV + 2 * D_GMLP

kernel_name = "hymba_gmlp_swa_convffn_encoder"


def _rmsnorm(x, g):
    xf = x.astype(jnp.float32)
    y = xf * lax.rsqrt(jnp.mean(xf * xf, axis=-1, keepdims=True) + EPS)
    return (y * g.astype(jnp.float32)).astype(x.dtype)


def _layernorm(x, g, b):
    xf = x.astype(jnp.float32)
    mu = jnp.mean(xf, axis=-1, keepdims=True)
    var = jnp.mean(jnp.square(xf - mu), axis=-1, keepdims=True)
    y = (xf - mu) * lax.rsqrt(var + EPS)
    return (y * g.astype(jnp.float32) + b.astype(jnp.float32)).astype(x.dtype)


def _partial_rope(x):
    L = x.shape[1]
    half = ROT_DIM // 2
    inv_freq = ROPE_THETA ** (-jnp.arange(0, ROT_DIM, 2, dtype=jnp.float32) / ROT_DIM)
    ang = jnp.arange(L, dtype=jnp.float32)[:, None] * inv_freq[None, :]
    cos = jnp.cos(ang)[None, :, None, :]
    sin = jnp.sin(ang)[None, :, None, :]
    xr = x[..., :ROT_DIM].astype(jnp.float32)
    x1, x2 = xr[..., :half], xr[..., half:]
    rot = jnp.concatenate([x1 * cos - x2 * sin, x2 * cos + x1 * sin], axis=-1)
    return jnp.concatenate([rot.astype(x.dtype), x[..., ROT_DIM:]], axis=-1)


def _windowed_gqa(q, k, v, sink):
    B, L = q.shape[0], q.shape[1]
    nb = L // BLOCK
    qb = q.reshape(B, nb, BLOCK, N_KV_HEADS, Q_PER_KV, HEAD_DIM)
    pad = ((0, 0), (1, 1), (0, 0), (0, 0), (0, 0))
    kp = jnp.pad(k.reshape(B, nb, BLOCK, N_KV_HEADS, HEAD_DIM), pad)
    vp = jnp.pad(v.reshape(B, nb, BLOCK, N_KV_HEADS, HEAD_DIM), pad)
    kb = jnp.concatenate([kp[:, :-2], kp[:, 1:-1], kp[:, 2:]], axis=2)
    vb = jnp.concatenate([vp[:, :-2], vp[:, 1:-1], vp[:, 2:]], axis=2)
    scale = HEAD_DIM ** -0.5
    scores = jnp.einsum('bnqkgd,bnskd->bnkgqs', qb, kb,
                        preferred_element_type=jnp.float32) * scale
    a = jnp.arange(BLOCK)[:, None]
    s = jnp.arange(3 * BLOCK)[None, :]
    band = jnp.abs(s - BLOCK - a) <= WINDOW
    kpos = jnp.arange(nb)[:, None] * BLOCK - BLOCK + jnp.arange(3 * BLOCK)[None, :]
    inside = (kpos >= 0) & (kpos < L)
    mask = band[None, :, :] & inside[:, None, :]
    scores = jnp.where(mask[None, :, None, None, :, :], scores, jnp.finfo(jnp.float32).min)
    sink_b = jnp.broadcast_to(sink.astype(jnp.float32).reshape(1, 1, N_KV_HEADS, Q_PER_KV, 1, 1),
                              scores.shape[:-1] + (1,))
    p = jax.nn.softmax(jnp.concatenate([scores, sink_b], axis=-1), axis=-1)[..., :-1]
    out = jnp.einsum('bnkgqs,bnskd->bnqkgd', p.astype(v.dtype), vb)
    return out.reshape(B, L, D_ATTN)


def _chunked_gmlp(u, v, ln_g, ln_b, w_s, b_s):
    B, L = u.shape[0], u.shape[1]
    nc = L // CHUNK
    u = jax.nn.gelu(u, approximate=False)
    v = _layernorm(jax.nn.gelu(v, approximate=False), ln_g, ln_b)
    vc = v.reshape(B, nc, CHUNK, N_GMLP_GROUPS, GMLP_GROUP_DIM)
    sg = jnp.einsum('gij,bcjgd->bcigd', w_s, vc) + b_s.T[None, None, :, :, None]
    out = u.reshape(B, nc, CHUNK, N_GMLP_GROUPS, GMLP_GROUP_DIM) * sg
    return out.reshape(B, L, D_GMLP)


def _dwconv3(h, w, b):
    hp = jnp.pad(h, ((0, 0), (1, 1), (0, 0)))
    return hp[:, :-2] * w[0] + hp[:, 1:-1] * w[1] + hp[:, 2:] * w[2] + b


def _layer(x, norm_mix_pre, w_in, attn_sink, gmlp_ln_g, gmlp_ln_b, gmlp_w_s, gmlp_b_s,
           out_norm_attn, out_norm_gmlp, w_o, norm_mix_post, norm_ffn_pre, w_ffn_in,
           conv_w, conv_b, w_ffn_out, norm_ffn_post):
    B, L, _ = x.shape
    h = _rmsnorm(x, norm_mix_pre)
    p = h @ w_in
    o1 = D_Q
    o2 = o1 + D_KV
    o3 = o2 + D_KV
    o4 = o3 + D_GMLP
    q = _partial_rope(p[..., :o1].reshape(B, L, N_Q_HEADS, HEAD_DIM))
    k = _partial_rope(p[..., o1:o2].reshape(B, L, N_KV_HEADS, HEAD_DIM))
    v = p[..., o2:o3].reshape(B, L, N_KV_HEADS, HEAD_DIM)
    attn = _windowed_gqa(q, k, v, attn_sink)
    gm = _chunked_gmlp(p[..., o3:o4], p[..., o4:], gmlp_ln_g, gmlp_ln_b, gmlp_w_s, gmlp_b_s)
    mixed = jnp.concatenate([_rmsnorm(attn, out_norm_attn), _rmsnorm(gm, out_norm_gmlp)], axis=-1)
    x = x + _rmsnorm(mixed @ w_o, norm_mix_post)
    h = _rmsnorm(x, norm_ffn_pre)
    up = _dwconv3(h @ w_ffn_in, conv_w, conv_b)
    f = jax.nn.silu(up[..., :D_FF]) * up[..., D_FF:]
    x = x + _rmsnorm(f @ w_ffn_out, norm_ffn_post)
    return x


def _trunk(x, norm_mix_pre, w_in, attn_sink, gmlp_ln_g, gmlp_ln_b, gmlp_w_s, gmlp_b_s,
           out_norm_attn, out_norm_gmlp, w_o, norm_mix_post, norm_ffn_pre, w_ffn_in,
           conv_w, conv_b, w_ffn_out, norm_ffn_post):
    for l in range(DEPTH):
        x = _layer(x, norm_mix_pre[l], w_in[l], attn_sink[l], gmlp_ln_g[l], gmlp_ln_b[l],
                   gmlp_w_s[l], gmlp_b_s[l], out_norm_attn[l], out_norm_gmlp[l], w_o[l],
                   norm_mix_post[l], norm_ffn_pre[l], w_ffn_in[l], conv_w[l], conv_b[l],
                   w_ffn_out[l], norm_ffn_post[l])
    return x


def setup_inputs(seed: int = 0) -> dict:
    key = jax.random.key(seed)
    ks = jax.random.split(key, 20)
    f32 = jnp.float32

    def nrm(k, shape, scale):
        return jax.random.normal(k, shape, f32) * scale

    def gain(k, shape):
        return 1.0 + 0.05 * jax.random.normal(k, shape, f32)

    return {
        "x_prompt": nrm(ks[0], (BATCH, SEQ, D_MODEL), 1.0),
        "x_sample": nrm(ks[1], (DEC_BATCH, DEC_SEQ, D_MODEL), 1.0),
        "norm_mix_pre": gain(ks[2], (DEPTH, D_MODEL)),
        "w_in": nrm(ks[3], (DEPTH, D_MODEL, D_IN_PROJ), D_MODEL ** -0.5),
        "attn_sink": nrm(ks[4], (DEPTH, N_Q_HEADS), 0.5),
        "gmlp_ln_g": gain(ks[5], (DEPTH, D_GMLP)),
        "gmlp_ln_b": nrm(ks[6], (DEPTH, D_GMLP), 0.02),
        "gmlp_w_s": nrm(ks[7], (DEPTH, N_GMLP_GROUPS, CHUNK, CHUNK), CHUNK ** -0.5),
        "gmlp_b_s": 1.0 + nrm(ks[8], (DEPTH, N_GMLP_GROUPS, CHUNK), 0.02),
        "out_norm_attn": gain(ks[9], (DEPTH, D_ATTN)),
        "out_norm_gmlp": gain(ks[10], (DEPTH, D_GMLP)),
        "w_o": nrm(ks[11], (DEPTH, D_MIX, D_MODEL), D_MIX ** -0.5),
        "norm_mix_post": gain(ks[12], (DEPTH, D_MODEL)),
        "norm_ffn_pre": gain(ks[13], (DEPTH, D_MODEL)),
        "w_ffn_in": nrm(ks[14], (DEPTH, D_MODEL, 2 * D_FF), D_MODEL ** -0.5),
        "conv_w": nrm(ks[15], (DEPTH, CONV_W, 2 * D_FF), CONV_W ** -0.5),
        "conv_b": nrm(ks[16], (DEPTH, 2 * D_FF), 0.02),
        "w_ffn_out": nrm(ks[17], (DEPTH, D_FF, D_MODEL), D_FF ** -0.5),
        "norm_ffn_post": gain(ks[18], (DEPTH, D_MODEL)),
    }


def reference(x_prompt, x_sample, norm_mix_pre, w_in, attn_sink, gmlp_ln_g, gmlp_ln_b,
              gmlp_w_s, gmlp_b_s, out_norm_attn, out_norm_gmlp, w_o, norm_mix_post,
              norm_ffn_pre, w_ffn_in, conv_w, conv_b, w_ffn_out, norm_ffn_post):
    y_prompt = _trunk(x_prompt, norm_mix_pre, w_in, attn_sink, gmlp_ln_g, gmlp_ln_b,
                      gmlp_w_s, gmlp_b_s, out_norm_attn, out_norm_gmlp, w_o, norm_mix_post,
                      norm_ffn_pre, w_ffn_in, conv_w, conv_b, w_ffn_out, norm_ffn_post)
    y_sample = _trunk(x_sample, norm_mix_pre, w_in, attn_sink, gmlp_ln_g, gmlp_ln_b,
                      gmlp_w_s, gmlp_b_s, out_norm_attn, out_norm_gmlp, w_o, norm_mix_post,
                      norm_ffn_pre, w_ffn_in, conv_w, conv_b, w_ffn_out, norm_ffn_post)
    return (y_prompt, y_sample)
```

```python
import functools

import numpy as np
import jax
import jax.numpy as jnp
from jax import lax
from jax.experimental import pallas as pl
from jax.experimental.pallas import tpu as pltpu

D_MODEL = 1024
D_ATTN = 512
D_GMLP = 512
HEAD_DIM = 64
N_Q_HEADS = 8
N_KV_HEADS = 2
ROT_DIM = 16
ROPE_THETA = 500000.0
WINDOW = 128
BLOCK = 128
N_GMLP_GROUPS = 8
CHUNK = 128
D_FF = 2816
EPS = 1e-6
D_Q = N_Q_HEADS * HEAD_DIM
D_KV = N_KV_HEADS * HEAD_DIM
D_IN_PROJ = D_Q + 2 * D_KV + 2 * D_GMLP

LANES = 128
BF16_ROWS = 16
FF_CHUNK = 256
N_FF_CHUNKS = D_FF // FF_CHUNK
VMEM_LIMIT = 56 * 1024 * 1024

PROJ_TILE = 512
MIX_TILE = 512
FFN_TILE = 512

F32 = jnp.float32
BF16 = jnp.bfloat16
SQRT_HALF = float(np.sqrt(0.5))
NEG = float(np.finfo(np.float32).min)


def _rms(x, g):
    ms = jnp.mean(x * x, axis=-1, keepdims=True)
    return x * lax.rsqrt(ms + EPS) * g


def _gelu(x):
    return 0.5 * x * (1.0 + lax.erf(x * SQRT_HALF))


def _proj_kernel(x_ref, g_ref, w_ref, rope_ref, lng_ref, lnb_ref,
                 q_ref, k4_ref, v4_ref, u_ref, vg2_ref):
    h = _rms(x_ref[...], g_ref[...]).astype(BF16)
    p = jnp.dot(h, w_ref[...], preferred_element_type=F32)

    cos = rope_ref[0]
    sin_up = rope_ref[1]
    sin_dn = rope_ref[2]

    def rope(t):
        return (t * cos + pltpu.roll(t, ROT_DIM // 2, 1) * sin_up
                + pltpu.roll(t, LANES - ROT_DIM // 2, 1) * sin_dn)

    scale = HEAD_DIM ** -0.5
    for i in range(D_Q // LANES):
        sl = slice(i * LANES, (i + 1) * LANES)
        q_ref[:, sl] = (rope(p[:, sl]) * scale).astype(BF16)

    lane = lax.broadcasted_iota(jnp.int32, (1, LANES), 1)
    first_head = lane < HEAD_DIM

    def spread(t, out_ref):
        sw = pltpu.roll(t, HEAD_DIM, 1)
        zero = jnp.zeros_like(t)
        parts = (jnp.where(first_head, t, zero), jnp.where(first_head, zero, sw),
                 jnp.where(first_head, sw, zero), jnp.where(first_head, zero, t))
        for i, part in enumerate(parts):
            out_ref[:, i * LANES:(i + 1) * LANES] = part.astype(BF16)

    spread(rope(p[:, D_Q:D_Q + D_KV]), k4_ref)
    spread(p[:, D_Q + D_KV:D_Q + 2 * D_KV], v4_ref)

    o3 = D_Q + 2 * D_KV
    u_ref[...] = _gelu(p[:, o3:o3 + D_GMLP]).astype(BF16)

    gv = _gelu(p[:, o3 + D_GMLP:])
    mu = jnp.mean(gv, axis=-1, keepdims=True)
    cen = gv - mu
    var = jnp.mean(cen * cen, axis=-1, keepdims=True)
    vn = cen * lax.rsqrt(var + EPS) * lng_ref[...] + lnb_ref[...]
    for i in range(D_GMLP // LANES):
        sl = slice(i * LANES, (i + 1) * LANES)
        t = vn[:, sl]
        zero = jnp.zeros_like(t)
        vg2_ref[:, sl] = jnp.where(first_head, t, zero).astype(BF16)
        vg2_ref[:, D_GMLP + i * LANES:D_GMLP + (i + 1) * LANES] = (
            jnp.where(first_head, zero, t).astype(BF16))


def _const_spec(shape):
    nd = len(shape)
    return pl.BlockSpec(shape, lambda b, j: (0,) * nd, pipeline_mode=pl.Buffered(1))


def _proj_call(x, g_pre, w_in, rope_tab, ln_g, ln_b):
    B, L, D = x.shape
    T = PROJ_TILE
    row = lambda width: pl.BlockSpec((None, T, width), lambda b, j: (b, j, 0))
    out_widths = (D_Q, 4 * LANES, 4 * LANES, D_GMLP, 2 * D_GMLP)
    return pl.pallas_call(
        _proj_kernel,
        grid=(B, L // T),
        in_specs=[
            row(D),
            _const_spec((1, D)),
            _const_spec((D, D_IN_PROJ)),
            pl.BlockSpec((3, T, LANES), lambda b, j: (0, j, 0)),
            _const_spec((1, D_GMLP)),
            _const_spec((1, D_GMLP)),
        ],
        out_specs=[row(w) for w in out_widths],
        out_shape=[jax.ShapeDtypeStruct((B, L, w), BF16) for w in out_widths],
        compiler_params=pltpu.CompilerParams(
            dimension_semantics=("arbitrary", "arbitrary"),
            vmem_limit_bytes=VMEM_LIMIT),
        name="proj",
    )(x, g_pre, w_in, rope_tab, ln_g, ln_b)


def _mix_kernel(n_seq_blocks, sink_ref, x_ref, q_ref, k_ref, kp_ref, kn_ref,
                v_ref, vp_ref, vn_ref, u_ref, vg2_ref, wsp_ref, bs_ref,
                ga_ref, gg_ref, wo_ref, gpost_ref, o_ref, kbuf, vbuf, mixed):
    T = x_ref.shape[0]
    nb = T // BLOCK
    j = pl.program_id(1)

    kbuf[0:BLOCK] = kp_ref[...]
    kbuf[BLOCK:BLOCK + T] = k_ref[...]
    kbuf[BLOCK + T:] = kn_ref[...]
    vbuf[0:BLOCK] = vp_ref[...]
    vbuf[BLOCK:BLOCK + T] = v_ref[...]
    vbuf[BLOCK + T:] = vn_ref[...]

    qi = lax.broadcasted_iota(jnp.int32, (2 * BLOCK, 3 * BLOCK), 0) % BLOCK
    ki = lax.broadcasted_iota(jnp.int32, (2 * BLOCK, 3 * BLOCK), 1)
    rel = ki - qi
    band = (rel >= 0) & (rel <= 2 * WINDOW)
    top_rows = lax.broadcasted_iota(jnp.int32, (2 * BLOCK, 1), 0) < BLOCK
    nt = (((1,), (1,)), ((), ()))

    for n in range(nb):
        blk = j * nb + n
        lo = jnp.where(blk > 0, 0, BLOCK)
        hi = jnp.where(blk < n_seq_blocks - 1, 3 * BLOCK, 2 * BLOCK)
        mask = band & (ki >= lo) & (ki < hi)
        rows = slice(n * BLOCK, (n + 1) * BLOCK)
        keys = slice(n * BLOCK, (n + 3) * BLOCK)

        for g in range(N_KV_HEADS):
            c0 = 2 * g * LANES
            lhs = jnp.concatenate(
                [q_ref[rows, c0:c0 + LANES], q_ref[rows, c0 + LANES:c0 + 2 * LANES]], axis=0)
            acc = None
            for half in range(2):
                cs = slice((2 * g + half) * LANES, (2 * g + half + 1) * LANES)
                s = lax.dot_general(lhs, kbuf[keys, cs], nt, preferred_element_type=F32)
                s = jnp.where(mask, s, NEG)
                sink = jnp.where(top_rows, sink_ref[4 * g + half], sink_ref[4 * g + 2 + half])
                m = jnp.maximum(jnp.max(s, axis=-1, keepdims=True), sink)
                e = jnp.exp(s - m)
                den = jnp.sum(e, axis=-1, keepdims=True) + jnp.exp(sink - m)
                pv = jnp.dot(e.astype(BF16), vbuf[keys, cs], preferred_element_type=F32)
                pv = pv * (1.0 / den)
                acc = pv if acc is None else acc + pv
            if g == 0:
                heads = [acc[0:BLOCK], acc[BLOCK:]]
            else:
                heads += [acc[0:BLOCK], acc[BLOCK:]]
        attn = jnp.concatenate(heads, axis=-1)
        mixed[rows, 0:D_ATTN] = _rms(attn, ga_ref[...]).astype(BF16)

        sgs = []
        for pair in range(N_GMLP_GROUPS // 2):
            ps = slice(pair * LANES, (pair + 1) * LANES)
            ps_hi = slice(D_GMLP + pair * LANES, D_GMLP + (pair + 1) * LANES)
            rhs = jnp.concatenate([vg2_ref[rows, ps], vg2_ref[rows, ps_hi]], axis=0)
            sgs.append(jnp.dot(wsp_ref[pair], rhs, preferred_element_type=F32))
        sg = jnp.concatenate(sgs, axis=-1) + bs_ref[...]
        gm = u_ref[rows, :].astype(F32) * sg
        mixed[rows, D_ATTN:] = _rms(gm, gg_ref[...]).astype(BF16)

    mm = jnp.dot(mixed[...], wo_ref[...], preferred_element_type=F32)
    o_ref[...] = x_ref[...] + _rms(mm, gpost_ref[...])


def _mix_call(x, q, k4, v4, u, vg2, sink, wsp, bs_tab, g_attn, g_gmlp, w_o, g_post):
    B, L, D = x.shape
    T = MIX_TILE
    nb = T // BLOCK
    n_seq_blocks = L // BLOCK
    row = lambda width: pl.BlockSpec((None, T, width), lambda b, j: (b, j, 0))
    prev = pl.BlockSpec((None, BLOCK, 4 * LANES),
                        lambda b, j: (b, jnp.maximum(j * nb - 1, 0), 0))
    nxt = pl.BlockSpec((None, BLOCK, 4 * LANES),
                       lambda b, j: (b, jnp.minimum((j + 1) * nb, n_seq_blocks - 1), 0))
    return pl.pallas_call(
        functools.partial(_mix_kernel, n_seq_blocks),
        grid=(B, L // T),
        in_specs=[
            pl.BlockSpec(memory_space=pltpu.SMEM),
            row(D), row(D_Q),
            row(4 * LANES), prev, nxt,
            row(4 * LANES), prev, nxt,
            row(D_GMLP), row(2 * D_GMLP),
            _const_spec((N_GMLP_GROUPS // 2, CHUNK, 2 * CHUNK)),
            _const_spec((CHUNK, D_GMLP)),
            _const_spec((1, D_ATTN)),
            _const_spec((1, D_GMLP)),
            _const_spec((D_ATTN + D_GMLP, D)),
            _const_spec((1, D)),
        ],
        out_specs=row(D),
        out_shape=jax.ShapeDtypeStruct((B, L, D), F32),
        scratch_shapes=[
            pltpu.VMEM((T + 2 * BLOCK, 4 * LANES), BF16),
            pltpu.VMEM((T + 2 * BLOCK, 4 * LANES), BF16),
            pltpu.VMEM((T, D_ATTN + D_GMLP), BF16),
        ],
        compiler_params=pltpu.CompilerParams(
            dimension_semantics=("arbitrary", "arbitrary"),
            vmem_limit_bytes=VMEM_LIMIT),
        name="mix",
    )(sink, x, q, k4, k4, k4, v4, v4, v4, u, vg2, wsp, bs_tab, g_attn, g_gmlp, w_o, g_post)


def _ffn_kernel(x_ref, xp_ref, xn_ref, gpre_ref, wg_ref, wu_ref, cg_ref, cu_ref,
                wd_ref, gpost_ref, o_ref, hbuf):
    T = x_ref.shape[0]
    H = BF16_ROWS
    j = pl.program_id(1)
    g = gpre_ref[...]

    hbuf[H:H + T] = _rms(x_ref[...], g).astype(BF16)
    hp = _rms(xp_ref[...], g)
    hbuf[0:H] = jnp.where(j > 0, hp, jnp.zeros_like(hp)).astype(BF16)
    hn = _rms(xn_ref[...], g)
    hbuf[H + T:] = jnp.where(j < pl.num_programs(1) - 1, hn, jnp.zeros_like(hn)).astype(BF16)

    def conv(t, c):
        before = pltpu.roll(t, 1, 0)
        after = pltpu.roll(t, T + 2 * H - 1, 0)
        r = before * c[0:1] + t * c[1:2] + after * c[2:3] + c[3:4]
        return r[H:H + T]

    acc = None
    for c in range(N_FF_CHUNKS):
        hb = hbuf[...]
        a = conv(jnp.dot(hb, wg_ref[c], preferred_element_type=F32), cg_ref[c])
        b = conv(jnp.dot(hb, wu_ref[c], preferred_element_type=F32), cu_ref[c])
        f = (a * jax.nn.sigmoid(a) * b).astype(BF16)
        d = jnp.dot(f, wd_ref[c], preferred_element_type=F32)
        acc = d if acc is None else acc + d
    o_ref[...] = x_ref[...] + _rms(acc, gpost_ref[...])


def _ffn_call(x, g_pre, wg, wu, cg, cu, wd, g_post):
    B, L, D = x.shape
    T = FFN_TILE
    H = BF16_ROWS
    per = T // H
    n_halo = L // H
    row = pl.BlockSpec((None, T, D), lambda b, j: (b, j, 0))
    prev = pl.BlockSpec((None, H, D), lambda b, j: (b, jnp.maximum(j * per - 1, 0), 0))
    nxt = pl.BlockSpec((None, H, D), lambda b, j: (b, jnp.minimum((j + 1) * per, n_halo - 1), 0))
    return pl.pallas_call(
        _ffn_kernel,
        grid=(B, L // T),
        in_specs=[
            row, prev, nxt,
            _const_spec((1, D)),
            _const_spec((N_FF_CHUNKS, D, FF_CHUNK)),
            _const_spec((N_FF_CHUNKS, D, FF_CHUNK)),
            _const_spec((N_FF_CHUNKS, 4, FF_CHUNK)),
            _const_spec((N_FF_CHUNKS, 4, FF_CHUNK)),
            _const_spec((N_FF_CHUNKS, FF_CHUNK, D)),
            _const_spec((1, D)),
        ],
        out_specs=row,
        out_shape=jax.ShapeDtypeStruct((B, L, D), F32),
        scratch_shapes=[pltpu.VMEM((T + 2 * H, D), BF16)],
        compiler_params=pltpu.CompilerParams(
            dimension_semantics=("arbitrary", "arbitrary"),
            vmem_limit_bytes=VMEM_LIMIT),
        name="ffn",
    )(x, x, x, g_pre, wg, wu, cg, cu, wd, g_post)


def _rope_table(L):
    half = ROT_DIM // 2
    inv_freq = ROPE_THETA ** (-jnp.arange(0, ROT_DIM, 2, dtype=F32) / ROT_DIM)
    ang = jnp.arange(L, dtype=F32)[:, None] * inv_freq[None, :]
    lane = np.arange(LANES) % HEAD_DIM
    idx = lane % half
    cos = jnp.cos(ang)[:, idx]
    sin = jnp.sin(ang)[:, idx]
    cos_t = jnp.where(lane < ROT_DIM, cos, 1.0)
    sin_up = jnp.where((lane >= half) & (lane < ROT_DIM), sin, 0.0)
    sin_dn = jnp.where(lane < half, -sin, 0.0)
    return jnp.stack([cos_t, sin_up, sin_dn]).astype(F32)


def _chunked_conv(w, b):
    t = jnp.concatenate([w, b[None, :]], axis=0)
    return t.reshape(4, N_FF_CHUNKS, FF_CHUNK).transpose(1, 0, 2)


def _layer(x, prm):
    L = x.shape[1]
    q, k4, v4, u, vg2 = _proj_call(x, prm["g_pre"], prm["w_in"], _rope_table(L),
                                   prm["ln_g"], prm["ln_b"])
    x = _mix_call(x, q, k4, v4, u, vg2, prm["sink"], prm["wsp"], prm["bs_tab"],
                  prm["g_attn"], prm["g_gmlp"], prm["w_o"], prm["g_mix_post"])
    return _ffn_call(x, prm["g_ffn_pre"], prm["wg"], prm["wu"], prm["cg"], prm["cu"],
                     prm["wd"], prm["g_ffn_post"])


def kernel(x_prompt, x_sample, norm_mix_pre, w_in, attn_sink, gmlp_ln_g, gmlp_ln_b, gmlp_w_s, gmlp_b_s, out_norm_attn, out_norm_gmlp, w_o, norm_mix_post, norm_ffn_pre, w_ffn_in, conv_w, conv_b, w_ffn_out, norm_ffn_post):
    depth = w_in.shape[0]
    for l in range(depth):
        ws = gmlp_w_s[l]
        wsp = jnp.concatenate([ws[0::2], ws[1::2]], axis=-1).astype(BF16)
        bs_tab = jnp.repeat(gmlp_b_s[l].T, D_GMLP // N_GMLP_GROUPS, axis=1)
        wfi = w_ffn_in[l].astype(BF16)
        to_chunks = lambda w: w.reshape(D_MODEL, N_FF_CHUNKS, FF_CHUNK).transpose(1, 0, 2)
        prm = dict(
            g_pre=norm_mix_pre[l][None, :],
            w_in=w_in[l].astype(BF16),
            ln_g=gmlp_ln_g[l][None, :], ln_b=gmlp_ln_b[l][None, :],
            sink=attn_sink[l],
            wsp=wsp, bs_tab=bs_tab,
            g_attn=out_norm_attn[l][None, :], g_gmlp=out_norm_gmlp[l][None, :],
            w_o=w_o[l].astype(BF16),
            g_mix_post=norm_mix_post[l][None, :],
            g_ffn_pre=norm_ffn_pre[l][None, :],
            wg=to_chunks(wfi[:, :D_FF]), wu=to_chunks(wfi[:, D_FF:]),
            cg=_chunked_conv(conv_w[l][:, :D_FF], conv_b[l][:D_FF]),
            cu=_chunked_conv(conv_w[l][:, D_FF:], conv_b[l][D_FF:]),
            wd=w_ffn_out[l].astype(BF16).reshape(N_FF_CHUNKS, FF_CHUNK, D_MODEL),
            g_ffn_post=norm_ffn_post[l][None, :],
        )
        x_prompt = _layer(x_prompt, prm)
        x_sample = _layer(x_sample, prm)
    return (x_prompt, x_sample)
```

```python
import functools

import numpy as np
import jax
import jax.numpy as jnp
from jax import lax
from jax.experimental import pallas as pl
from jax.experimental.pallas import tpu as pltpu

D_MODEL = 1024
D_ATTN = 512
D_GMLP = 512
HEAD_DIM = 64
N_Q_HEADS = 8
N_KV_HEADS = 2
ROT_DIM = 16
ROPE_THETA = 500000.0
WINDOW = 128
BLOCK = 128
N_GMLP_GROUPS = 8
CHUNK = 128
D_FF = 2816
EPS = 1e-6
D_Q = N_Q_HEADS * HEAD_DIM
D_KV = N_KV_HEADS * HEAD_DIM
D_IN_PROJ = D_Q + 2 * D_KV + 2 * D_GMLP

LANES = 128
BF16_ROWS = 16
FF_CHUNK = 256
N_FF_CHUNKS = D_FF // FF_CHUNK
VMEM_LIMIT = 56 * 1024 * 1024

PROJ_TILE = 512
MIX_TILE = 512
FFN_TILE = 512

F32 = jnp.float32
BF16 = jnp.bfloat16
SQRT_HALF = float(np.sqrt(0.5))
LOG2E = float(np.log2(np.e))
NEG = float(np.finfo(np.float32).min)


def _rms(x, g):
    ms = jnp.mean(x * x, axis=-1, keepdims=True)
    return x * lax.rsqrt(ms + EPS) * g


def _gelu(x):
    return 0.5 * x * (1.0 + lax.erf(x * SQRT_HALF))


def _proj_kernel(x_ref, g_ref, w_ref, rope_ref, lng_ref, lnb_ref,
                 q_ref, k4_ref, v4_ref, u_ref, vg2_ref, hbuf):
    hbuf[...] = _rms(x_ref[...], g_ref[...]).astype(BF16)

    def project(lo, hi):
        return jnp.dot(hbuf[...], w_ref[:, lo:hi], preferred_element_type=F32)

    cos = rope_ref[0]
    sin_up = rope_ref[1]
    sin_dn = rope_ref[2]
    lane = lax.broadcasted_iota(jnp.int32, (1, LANES), 1)
    first_head = lane < HEAD_DIM

    def rope(t):
        return (t * cos + pltpu.roll(t, ROT_DIM // 2, 1) * sin_up
                + pltpu.roll(t, LANES - ROT_DIM // 2, 1) * sin_dn)

    def spread(t, out_ref):
        sw = pltpu.roll(t, HEAD_DIM, 1)
        zero = jnp.zeros_like(t)
        parts = (jnp.where(first_head, t, zero), jnp.where(first_head, zero, sw),
                 jnp.where(first_head, sw, zero), jnp.where(first_head, zero, t))
        for i, part in enumerate(parts):
            out_ref[:, i * LANES:(i + 1) * LANES] = part.astype(BF16)

    def finish_gate(p):
        gv = _gelu(p)
        mu = jnp.mean(gv, axis=-1, keepdims=True)
        cen = gv - mu
        var = jnp.mean(cen * cen, axis=-1, keepdims=True)
        vn = cen * lax.rsqrt(var + EPS) * lng_ref[...] + lnb_ref[...]
        for i in range(D_GMLP // LANES):
            t = vn[:, i * LANES:(i + 1) * LANES]
            zero = jnp.zeros_like(t)
            vg2_ref[:, i * LANES:(i + 1) * LANES] = jnp.where(first_head, t, zero).astype(BF16)
            vg2_ref[:, D_GMLP + i * LANES:D_GMLP + (i + 1) * LANES] = (
                jnp.where(first_head, zero, t).astype(BF16))

    def finish_kv(p):
        spread(rope(p[:, 0:D_KV]), k4_ref)
        spread(p[:, D_KV:], v4_ref)

    def finish_q(p):
        scale = HEAD_DIM ** -0.5 * LOG2E
        for i in range(D_Q // LANES):
            sl = slice(i * LANES, (i + 1) * LANES)
            q_ref[:, sl] = (rope(p[:, sl]) * scale).astype(BF16)

    o3 = D_Q + 2 * D_KV
    p_q = project(0, D_Q)
    p_kv = project(D_Q, o3)
    p_gate = project(o3 + D_GMLP, D_IN_PROJ)
    finish_q(p_q)
    finish_kv(p_kv)
    p_u = project(o3, o3 + D_GMLP)
    finish_gate(p_gate)
    u_ref[...] = _gelu(p_u).astype(BF16)


def _const_spec(shape):
    nd = len(shape)
    return pl.BlockSpec(shape, lambda b, j: (0,) * nd, pipeline_mode=pl.Buffered(1))


def _proj_call(x, g_pre, w_in, rope_tab, ln_g, ln_b):
    B, L, D = x.shape
    T = PROJ_TILE
    row = lambda width: pl.BlockSpec((None, T, width), lambda b, j: (b, j, 0))
    out_widths = (D_Q, 4 * LANES, 4 * LANES, D_GMLP, 2 * D_GMLP)
    return pl.pallas_call(
        _proj_kernel,
        grid=(B, L // T),
        in_specs=[
            row(D),
            _const_spec((1, D)),
            _const_spec((D, D_IN_PROJ)),
            pl.BlockSpec((3, T, LANES), lambda b, j: (0, j, 0)),
            _const_spec((1, D_GMLP)),
            _const_spec((1, D_GMLP)),
        ],
        out_specs=[row(w) for w in out_widths],
        out_shape=[jax.ShapeDtypeStruct((B, L, w), BF16) for w in out_widths],
        scratch_shapes=[pltpu.VMEM((T, D), BF16)],
        compiler_params=pltpu.CompilerParams(
            dimension_semantics=("arbitrary", "arbitrary"),
            vmem_limit_bytes=VMEM_LIMIT),
        name="proj",
    )(x, g_pre, w_in, rope_tab, ln_g, ln_b)


def _mix_kernel(n_seq_blocks, sink_ref, x_ref, q_ref, k_ref, kp_ref, kn_ref,
                v_ref, vp_ref, vn_ref, u_ref, vg2_ref, wsp_ref, bs_ref,
                ga_ref, gg_ref, wo_ref, gpost_ref, o_ref, kbuf, vbuf, mixed):
    T = x_ref.shape[0]
    nb = T // BLOCK
    j = pl.program_id(1)

    kbuf[0:BLOCK] = kp_ref[...]
    kbuf[BLOCK:BLOCK + T] = k_ref[...]
    kbuf[BLOCK + T:] = kn_ref[...]
    vbuf[0:BLOCK] = vp_ref[...]
    vbuf[BLOCK:BLOCK + T] = v_ref[...]
    vbuf[BLOCK + T:] = vn_ref[...]

    qi = lax.broadcasted_iota(jnp.int32, (2 * BLOCK, BLOCK), 0) % BLOCK
    ki = lax.broadcasted_iota(jnp.int32, (2 * BLOCK, BLOCK), 1)
    neg = jnp.full((2 * BLOCK, BLOCK), NEG, F32)
    zero = jnp.zeros((2 * BLOCK, BLOCK), F32)
    bias_prev = jnp.where(ki >= qi, zero, neg)
    bias_next = jnp.where(ki <= qi, zero, neg)
    top_rows = lax.broadcasted_iota(jnp.int32, (2 * BLOCK, 1), 0) < BLOCK
    nt = (((1,), (1,)), ((), ()))

    def biases(n):
        blk = j * nb + n
        return (jnp.where(blk > 0, bias_prev, neg),
                jnp.where(blk < n_seq_blocks - 1, bias_next, neg))

    def scores(n, g, half):
        rows = slice(n * BLOCK, (n + 1) * BLOCK)
        keys = slice(n * BLOCK, (n + 3) * BLOCK)
        c0 = 2 * g * LANES
        lhs = jnp.concatenate(
            [q_ref[rows, c0:c0 + LANES], q_ref[rows, c0 + LANES:c0 + 2 * LANES]], axis=0)
        cs = slice((2 * g + half) * LANES, (2 * g + half + 1) * LANES)
        return lax.dot_general(lhs, kbuf[keys, cs], nt, preferred_element_type=F32)

    def softmax(s, g, half, bias_p, bias_n):
        s0 = s[:, 0:BLOCK] + bias_p
        s1 = s[:, BLOCK:2 * BLOCK]
        s2 = s[:, 2 * BLOCK:] + bias_n
        sink = jnp.where(top_rows, sink_ref[4 * g + half], sink_ref[4 * g + 2 + half]) * LOG2E
        m = jnp.max(jnp.maximum(jnp.maximum(s0, s1), s2), axis=-1, keepdims=True)
        m = jnp.maximum(m, sink)
        e0 = jnp.exp2(s0 - m)
        e1 = jnp.exp2(s1 - m)
        e2 = jnp.exp2(s2 - m)
        den = jnp.sum(e0 + e1 + e2, axis=-1, keepdims=True) + jnp.exp2(sink - m)
        return jnp.concatenate([e0, e1, e2], axis=-1).astype(BF16), 1.0 / den

    def weighted_values(e, rden, n, g, half):
        keys = slice(n * BLOCK, (n + 3) * BLOCK)
        cs = slice((2 * g + half) * LANES, (2 * g + half + 1) * LANES)
        return jnp.dot(e, vbuf[keys, cs], preferred_element_type=F32) * rden

    def finish_block(n, outs):
        rows = slice(n * BLOCK, (n + 1) * BLOCK)
        a0 = outs[(0, 0)] + outs[(0, 1)]
        a1 = outs[(1, 0)] + outs[(1, 1)]
        attn = jnp.concatenate([a0[0:BLOCK], a0[BLOCK:], a1[0:BLOCK], a1[BLOCK:]], axis=-1)
        mixed[rows, 0:D_ATTN] = _rms(attn, ga_ref[...]).astype(BF16)
        sgs = []
        for pair in range(N_GMLP_GROUPS // 2):
            ps = slice(pair * LANES, (pair + 1) * LANES)
            ps_hi = slice(D_GMLP + pair * LANES, D_GMLP + (pair + 1) * LANES)
            rhs = jnp.concatenate([vg2_ref[rows, ps], vg2_ref[rows, ps_hi]], axis=0)
            sgs.append(jnp.dot(wsp_ref[pair], rhs, preferred_element_type=F32))
        sg = jnp.concatenate(sgs, axis=-1) + bs_ref[...]
        gm = u_ref[rows, :].astype(F32) * sg
        mixed[rows, D_ATTN:] = _rms(gm, gg_ref[...]).astype(BF16)

    chains = [(n, g, half) for n in range(nb) for g in range(N_KV_HEADS) for half in range(2)]
    last_of_block = (N_KV_HEADS - 1, 1)
    outs = {}

    def retire(e, rden, n, g, half):
        outs[(g, half)] = weighted_values(e, rden, n, g, half)
        if (g, half) == last_of_block:
            finish_block(n, outs)
            outs.clear()

    next_scores = scores(*chains[0])
    waiting = None
    bias_pn = biases(0)
    for i, (n, g, half) in enumerate(chains):
        s = next_scores
        if i + 1 < len(chains):
            next_scores = scores(*chains[i + 1])
        e, rden = softmax(s, g, half, *bias_pn)
        if waiting is not None:
            retire(*waiting)
        waiting = (e, rden, n, g, half)
        if (g, half) == last_of_block and n + 1 < nb:
            bias_pn = biases(n + 1)
    retire(*waiting)

    mm = jnp.dot(mixed[...], wo_ref[...], preferred_element_type=F32)
    o_ref[...] = x_ref[...] + _rms(mm, gpost_ref[...])


def _mix_call(x, q, k4, v4, u, vg2, sink, wsp, bs_tab, g_attn, g_gmlp, w_o, g_post):
    B, L, D = x.shape
    T = MIX_TILE
    nb = T // BLOCK
    n_seq_blocks = L // BLOCK
    row = lambda width: pl.BlockSpec((None, T, width), lambda b, j: (b, j, 0))
    prev = pl.BlockSpec((None, BLOCK, 4 * LANES),
                        lambda b, j: (b, jnp.maximum(j * nb - 1, 0), 0))
    nxt = pl.BlockSpec((None, BLOCK, 4 * LANES),
                       lambda b, j: (b, jnp.minimum((j + 1) * nb, n_seq_blocks - 1), 0))
    return pl.pallas_call(
        functools.partial(_mix_kernel, n_seq_blocks),
        grid=(B, L // T),
        in_specs=[
            pl.BlockSpec(memory_space=pltpu.SMEM),
            row(D), row(D_Q),
            row(4 * LANES), prev, nxt,
            row(4 * LANES), prev, nxt,
            row(D_GMLP), row(2 * D_GMLP),
            _const_spec((N_GMLP_GROUPS // 2, CHUNK, 2 * CHUNK)),
            _const_spec((CHUNK, D_GMLP)),
            _const_spec((1, D_ATTN)),
            _const_spec((1, D_GMLP)),
            _const_spec((D_ATTN + D_GMLP, D)),
            _const_spec((1, D)),
        ],
        out_specs=row(D),
        out_shape=jax.ShapeDtypeStruct((B, L, D), F32),
        scratch_shapes=[
            pltpu.VMEM((T + 2 * BLOCK, 4 * LANES), BF16),
            pltpu.VMEM((T + 2 * BLOCK, 4 * LANES), BF16),
            pltpu.VMEM((T, D_ATTN + D_GMLP), BF16),
        ],
        compiler_params=pltpu.CompilerParams(
            dimension_semantics=("arbitrary", "arbitrary"),
            vmem_limit_bytes=VMEM_LIMIT),
        name="mix",
    )(sink, x, q, k4, k4, k4, v4, v4, v4, u, vg2, wsp, bs_tab, g_attn, g_gmlp, w_o, g_post)


def _ffn_kernel(x_ref, xp_ref, xn_ref, gpre_ref, wg_ref, wu_ref, cg_ref, cu_ref,
                wd_ref, gpost_ref, o_ref, hbuf, fbuf):
    T = x_ref.shape[0]
    H = BF16_ROWS
    j = pl.program_id(1)
    g = gpre_ref[...]

    hbuf[H:H + T] = _rms(x_ref[...], g).astype(BF16)
    hp = _rms(xp_ref[...], g)
    hbuf[0:H] = jnp.where(j > 0, hp, jnp.zeros_like(hp)).astype(BF16)
    hn = _rms(xn_ref[...], g)
    hbuf[H + T:] = jnp.where(j < pl.num_programs(1) - 1, hn, jnp.zeros_like(hn)).astype(BF16)

    def conv(t, c):
        before = pltpu.roll(t, 1, 0)
        after = pltpu.roll(t, T + 2 * H - 1, 0)
        r = before * c[0:1] + t * c[1:2] + after * c[2:3] + c[3:4]
        return r[H:H + T]

    for c in range(N_FF_CHUNKS):
        a = conv(jnp.dot(hbuf[...], wg_ref[c], preferred_element_type=F32), cg_ref[c])
        b = conv(jnp.dot(hbuf[...], wu_ref[c], preferred_element_type=F32), cu_ref[c])
        fbuf[:, c * FF_CHUNK:(c + 1) * FF_CHUNK] = (a * jax.nn.sigmoid(a) * b).astype(BF16)
    acc = jnp.dot(fbuf[...], wd_ref[...], preferred_element_type=F32)
    o_ref[...] = x_ref[...] + _rms(acc, gpost_ref[...])


def _ffn_call(x, g_pre, wg, wu, cg, cu, wd, g_post):
    B, L, D = x.shape
    T = FFN_TILE
    H = BF16_ROWS
    per = T // H
    n_halo = L // H
    row = pl.BlockSpec((None, T, D), lambda b, j: (b, j, 0))
    prev = pl.BlockSpec((None, H, D), lambda b, j: (b, jnp.maximum(j * per - 1, 0), 0))
    nxt = pl.BlockSpec((None, H, D), lambda b, j: (b, jnp.minimum((j + 1) * per, n_halo - 1), 0))
    return pl.pallas_call(
        _ffn_kernel,
        grid=(B, L // T),
        in_specs=[
            row, prev, nxt,
            _const_spec((1, D)),
            _const_spec((N_FF_CHUNKS, D, FF_CHUNK)),
            _const_spec((N_FF_CHUNKS, D, FF_CHUNK)),
            _const_spec((N_FF_CHUNKS, 4, FF_CHUNK)),
            _const_spec((N_FF_CHUNKS, 4, FF_CHUNK)),
            _const_spec((D_FF, D)),
            _const_spec((1, D)),
        ],
        out_specs=row,
        out_shape=jax.ShapeDtypeStruct((B, L, D), F32),
        scratch_shapes=[pltpu.VMEM((T + 2 * H, D), BF16), pltpu.VMEM((T, D_FF), BF16)],
        compiler_params=pltpu.CompilerParams(
            dimension_semantics=("arbitrary", "arbitrary"),
            vmem_limit_bytes=VMEM_LIMIT),
        name="ffn",
    )(x, x, x, g_pre, wg, wu, cg, cu, wd, g_post)


def _rope_table(L):
    half = ROT_DIM // 2
    inv_freq = ROPE_THETA ** (-jnp.arange(0, ROT_DIM, 2, dtype=F32) / ROT_DIM)
    ang = jnp.arange(L, dtype=F32)[:, None] * inv_freq[None, :]
    lane = np.arange(LANES) % HEAD_DIM
    idx = lane % half
    cos = jnp.cos(ang)[:, idx]
    sin = jnp.sin(ang)[:, idx]
    cos_t = jnp.where(lane < ROT_DIM, cos, 1.0)
    sin_up = jnp.where((lane >= half) & (lane < ROT_DIM), sin, 0.0)
    sin_dn = jnp.where(lane < half, -sin, 0.0)
    return jnp.stack([cos_t, sin_up, sin_dn]).astype(F32)


def _chunked_conv(w, b):
    t = jnp.concatenate([w, b[None, :]], axis=0)
    return t.reshape(4, N_FF_CHUNKS, FF_CHUNK).transpose(1, 0, 2)


def _layer(x, prm):
    L = x.shape[1]
    q, k4, v4, u, vg2 = _proj_call(x, prm["g_pre"], prm["w_in"], _rope_table(L),
                                   prm["ln_g"], prm["ln_b"])
    x = _mix_call(x, q, k4, v4, u, vg2, prm["sink"], prm["wsp"], prm["bs_tab"],
                  prm["g_attn"], prm["g_gmlp"], prm["w_o"], prm["g_mix_post"])
    return _ffn_call(x, prm["g_ffn_pre"], prm["wg"], prm["wu"], prm["cg"], prm["cu"],
                     prm["wd"], prm["g_ffn_post"])


def kernel(x_prompt, x_sample, norm_mix_pre, w_in, attn_sink, gmlp_ln_g, gmlp_ln_b, gmlp_w_s, gmlp_b_s, out_norm_attn, out_norm_gmlp, w_o, norm_mix_post, norm_ffn_pre, w_ffn_in, conv_w, conv_b, w_ffn_out, norm_ffn_post):
    depth = w_in.shape[0]
    for l in range(depth):
        ws = gmlp_w_s[l]
        wsp = jnp.concatenate([ws[0::2], ws[1::2]], axis=-1).astype(BF16)
        bs_tab = jnp.repeat(gmlp_b_s[l].T, D_GMLP // N_GMLP_GROUPS, axis=1)
        wfi = w_ffn_in[l].astype(BF16)
        to_chunks = lambda w: w.reshape(D_MODEL, N_FF_CHUNKS, FF_CHUNK).transpose(1, 0, 2)
        prm = dict(
            g_pre=norm_mix_pre[l][None, :],
            w_in=w_in[l].astype(BF16),
            ln_g=gmlp_ln_g[l][None, :], ln_b=gmlp_ln_b[l][None, :],
            sink=attn_sink[l],
            wsp=wsp, bs_tab=bs_tab,
            g_attn=out_norm_attn[l][None, :], g_gmlp=out_norm_gmlp[l][None, :],
            w_o=w_o[l].astype(BF16),
            g_mix_post=norm_mix_post[l][None, :],
            g_ffn_pre=norm_ffn_pre[l][None, :],
            wg=to_chunks(wfi[:, :D_FF]), wu=to_chunks(wfi[:, D_FF:]),
            cg=_chunked_conv(conv_w[l][:, :D_FF], conv_b[l][:D_FF]),
            cu=_chunked_conv(conv_w[l][:, D_FF:], conv_b[l][D_FF:]),
            wd=w_ffn_out[l].astype(BF16),
            g_ffn_post=norm_ffn_post[l][None, :],
        )
        x_prompt = _layer(x_prompt, prm)
        x_sample = _layer(x_sample, prm)
    return (x_prompt, x_sample)
```

```python
import functools

import numpy as np
import jax
import jax.numpy as jnp
from jax import lax
from jax.experimental import pallas as pl
from jax.experimental.pallas import tpu as pltpu

D_MODEL = 1024
D_ATTN = 512
D_GMLP = 512
HEAD_DIM = 64
N_Q_HEADS = 8
N_KV_HEADS = 2
ROT_DIM = 16
ROPE_THETA = 500000.0
WINDOW = 128
BLOCK = 128
N_GMLP_GROUPS = 8
CHUNK = 128
D_FF = 2816
EPS = 1e-6
D_Q = N_Q_HEADS * HEAD_DIM
D_KV = N_KV_HEADS * HEAD_DIM
D_IN_PROJ = D_Q + 2 * D_KV + 2 * D_GMLP

LANES = 128
BF16_ROWS = 16
FF_CHUNK = 256
N_FF_CHUNKS = D_FF // FF_CHUNK
VMEM_LIMIT = 56 * 1024 * 1024

PROJ_TILE = 1024
MIX_TILE = 1024
FFN_TILE = 1024

F32 = jnp.float32
BF16 = jnp.bfloat16
SQRT_HALF = float(np.sqrt(0.5))
LOG2E = float(np.log2(np.e))
NEG = float(np.finfo(np.float32).min)


def _rms(x, g):
    ms = jnp.mean(x * x, axis=-1, keepdims=True)
    return x * lax.rsqrt(ms + EPS) * g


def _gelu(x):
    return 0.5 * x * (1.0 + lax.erf(x * SQRT_HALF))


def _proj_kernel(x_ref, g_ref, w_ref, rope_ref, lng_ref, lnb_ref,
                 q_ref, k4_ref, v4_ref, u_ref, vg2_ref, hbuf):
    hbuf[...] = _rms(x_ref[...], g_ref[...]).astype(BF16)

    def project(lo, hi):
        return jnp.dot(hbuf[...], w_ref[:, lo:hi], preferred_element_type=F32)

    cos = rope_ref[0]
    sin_up = rope_ref[1]
    sin_dn = rope_ref[2]
    lane = lax.broadcasted_iota(jnp.int32, (1, LANES), 1)
    first_head = lane < HEAD_DIM

    def rope(t):
        return (t * cos + pltpu.roll(t, ROT_DIM // 2, 1) * sin_up
                + pltpu.roll(t, LANES - ROT_DIM // 2, 1) * sin_dn)

    def spread(t, out_ref):
        sw = pltpu.roll(t, HEAD_DIM, 1)
        zero = jnp.zeros_like(t)
        parts = (jnp.where(first_head, t, zero), jnp.where(first_head, zero, sw),
                 jnp.where(first_head, sw, zero), jnp.where(first_head, zero, t))
        for i, part in enumerate(parts):
            out_ref[:, i * LANES:(i + 1) * LANES] = part.astype(BF16)

    def finish_gate(p):
        gv = _gelu(p)
        mu = jnp.mean(gv, axis=-1, keepdims=True)
        cen = gv - mu
        var = jnp.mean(cen * cen, axis=-1, keepdims=True)
        vn = cen * lax.rsqrt(var + EPS) * lng_ref[...] + lnb_ref[...]
        for i in range(D_GMLP // LANES):
            t = vn[:, i * LANES:(i + 1) * LANES]
            zero = jnp.zeros_like(t)
            vg2_ref[:, i * LANES:(i + 1) * LANES] = jnp.where(first_head, t, zero).astype(BF16)
            vg2_ref[:, D_GMLP + i * LANES:D_GMLP + (i + 1) * LANES] = (
                jnp.where(first_head, zero, t).astype(BF16))

    def finish_kv(p):
        spread(rope(p[:, 0:D_KV]), k4_ref)
        spread(p[:, D_KV:], v4_ref)

    def finish_q(p):
        scale = HEAD_DIM ** -0.5 * LOG2E
        for i in range(D_Q // LANES):
            sl = slice(i * LANES, (i + 1) * LANES)
            q_ref[:, sl] = (rope(p[:, sl]) * scale).astype(BF16)

    o3 = D_Q + 2 * D_KV
    p_q = project(0, D_Q)
    p_kv = project(D_Q, o3)
    p_gate = project(o3 + D_GMLP, D_IN_PROJ)
    finish_q(p_q)
    finish_kv(p_kv)
    p_u = project(o3, o3 + D_GMLP)
    finish_gate(p_gate)
    u_ref[...] = _gelu(p_u).astype(BF16)


def _const_spec(shape):
    nd = len(shape)
    return pl.BlockSpec(shape, lambda b, j: (0,) * nd, pipeline_mode=pl.Buffered(1))


def _proj_call(x, g_pre, w_in, rope_tab, ln_g, ln_b):
    B, L, D = x.shape
    T = PROJ_TILE
    row = lambda width: pl.BlockSpec((None, T, width), lambda b, j: (b, j, 0))
    out_widths = (D_Q, 4 * LANES, 4 * LANES, D_GMLP, 2 * D_GMLP)
    return pl.pallas_call(
        _proj_kernel,
        grid=(B, L // T),
        in_specs=[
            row(D),
            _const_spec((1, D)),
            _const_spec((D, D_IN_PROJ)),
            pl.BlockSpec((3, T, LANES), lambda b, j: (0, j, 0)),
            _const_spec((1, D_GMLP)),
            _const_spec((1, D_GMLP)),
        ],
        out_specs=[row(w) for w in out_widths],
        out_shape=[jax.ShapeDtypeStruct((B, L, w), BF16) for w in out_widths],
        scratch_shapes=[pltpu.VMEM((T, D), BF16)],
        compiler_params=pltpu.CompilerParams(
            dimension_semantics=("arbitrary", "arbitrary"),
            vmem_limit_bytes=VMEM_LIMIT),
        name="proj",
    )(x, g_pre, w_in, rope_tab, ln_g, ln_b)


def _mix_kernel(n_seq_blocks, sink_ref, x_ref, q_ref, k_ref, kp_ref, kn_ref,
                v_ref, vp_ref, vn_ref, u_ref, vg2_ref, wsp_ref, bs_ref,
                ga_ref, gg_ref, wo_ref, gpost_ref, o_ref, kbuf, vbuf, mixed):
    T = x_ref.shape[0]
    nb = T // BLOCK
    j = pl.program_id(1)

    kbuf[0:BLOCK] = kp_ref[...]
    kbuf[BLOCK:BLOCK + T] = k_ref[...]
    kbuf[BLOCK + T:] = kn_ref[...]
    vbuf[0:BLOCK] = vp_ref[...]
    vbuf[BLOCK:BLOCK + T] = v_ref[...]
    vbuf[BLOCK + T:] = vn_ref[...]

    qi = lax.broadcasted_iota(jnp.int32, (2 * BLOCK, BLOCK), 0) % BLOCK
    ki = lax.broadcasted_iota(jnp.int32, (2 * BLOCK, BLOCK), 1)
    neg = jnp.full((2 * BLOCK, BLOCK), NEG, F32)
    zero = jnp.zeros((2 * BLOCK, BLOCK), F32)
    bias_prev = jnp.where(ki >= qi, zero, neg)
    bias_next = jnp.where(ki <= qi, zero, neg)
    top_rows = lax.broadcasted_iota(jnp.int32, (2 * BLOCK, 1), 0) < BLOCK
    nt = (((1,), (1,)), ((), ()))

    def biases(n):
        blk = j * nb + n
        return (jnp.where(blk > 0, bias_prev, neg),
                jnp.where(blk < n_seq_blocks - 1, bias_next, neg))

    def scores(n, g, half):
        rows = slice(n * BLOCK, (n + 1) * BLOCK)
        keys = slice(n * BLOCK, (n + 3) * BLOCK)
        c0 = 2 * g * LANES
        lhs = jnp.concatenate(
            [q_ref[rows, c0:c0 + LANES], q_ref[rows, c0 + LANES:c0 + 2 * LANES]], axis=0)
        cs = slice((2 * g + half) * LANES, (2 * g + half + 1) * LANES)
        return lax.dot_general(lhs, kbuf[keys, cs], nt, preferred_element_type=F32)

    def softmax(s, g, half, bias_p, bias_n):
        s0 = s[:, 0:BLOCK] + bias_p
        s1 = s[:, BLOCK:2 * BLOCK]
        s2 = s[:, 2 * BLOCK:] + bias_n
        sink = jnp.where(top_rows, sink_ref[4 * g + half], sink_ref[4 * g + 2 + half]) * LOG2E
        m = jnp.max(jnp.maximum(jnp.maximum(s0, s1), s2), axis=-1, keepdims=True)
        m = jnp.maximum(m, sink)
        e0 = jnp.exp2(s0 - m)
        e1 = jnp.exp2(s1 - m)
        e2 = jnp.exp2(s2 - m)
        den = jnp.sum(e0 + e1 + e2, axis=-1, keepdims=True) + jnp.exp2(sink - m)
        return jnp.concatenate([e0, e1, e2], axis=-1).astype(BF16), 1.0 / den

    def weighted_values(e, rden, n, g, half):
        keys = slice(n * BLOCK, (n + 3) * BLOCK)
        cs = slice((2 * g + half) * LANES, (2 * g + half + 1) * LANES)
        return jnp.dot(e, vbuf[keys, cs], preferred_element_type=F32) * rden

    def finish_block(n, outs):
        rows = slice(n * BLOCK, (n + 1) * BLOCK)
        a0 = outs[(0, 0)] + outs[(0, 1)]
        a1 = outs[(1, 0)] + outs[(1, 1)]
        attn = jnp.concatenate([a0[0:BLOCK], a0[BLOCK:], a1[0:BLOCK], a1[BLOCK:]], axis=-1)
        mixed[rows, 0:D_ATTN] = _rms(attn, ga_ref[...]).astype(BF16)
        sgs = []
        for pair in range(N_GMLP_GROUPS // 2):
            ps = slice(pair * LANES, (pair + 1) * LANES)
            ps_hi = slice(D_GMLP + pair * LANES, D_GMLP + (pair + 1) * LANES)
            rhs = jnp.concatenate([vg2_ref[rows, ps], vg2_ref[rows, ps_hi]], axis=0)
            sgs.append(jnp.dot(wsp_ref[pair], rhs, preferred_element_type=F32))
        sg = jnp.concatenate(sgs, axis=-1) + bs_ref[...]
        gm = u_ref[rows, :].astype(F32) * sg
        mixed[rows, D_ATTN:] = _rms(gm, gg_ref[...]).astype(BF16)

    chains = [(n, g, half) for n in range(nb) for g in range(N_KV_HEADS) for half in range(2)]
    last_of_block = (N_KV_HEADS - 1, 1)
    outs = {}

    def retire(e, rden, n, g, half):
        outs[(g, half)] = weighted_values(e, rden, n, g, half)
        if (g, half) == last_of_block:
            finish_block(n, outs)
            outs.clear()

    next_scores = scores(*chains[0])
    waiting = None
    bias_pn = biases(0)
    for i, (n, g, half) in enumerate(chains):
        s = next_scores
        if i + 1 < len(chains):
            next_scores = scores(*chains[i + 1])
        e, rden = softmax(s, g, half, *bias_pn)
        if waiting is not None:
            retire(*waiting)
        waiting = (e, rden, n, g, half)
        if (g, half) == last_of_block and n + 1 < nb:
            bias_pn = biases(n + 1)
    retire(*waiting)

    mm = jnp.dot(mixed[...], wo_ref[...], preferred_element_type=F32)
    o_ref[...] = x_ref[...] + _rms(mm, gpost_ref[...])


def _mix_call(x, q, k4, v4, u, vg2, sink, wsp, bs_tab, g_attn, g_gmlp, w_o, g_post):
    B, L, D = x.shape
    T = MIX_TILE
    nb = T // BLOCK
    n_seq_blocks = L // BLOCK
    row = lambda width: pl.BlockSpec((None, T, width), lambda b, j: (b, j, 0))
    prev = pl.BlockSpec((None, BLOCK, 4 * LANES),
                        lambda b, j: (b, jnp.maximum(j * nb - 1, 0), 0))
    nxt = pl.BlockSpec((None, BLOCK, 4 * LANES),
                       lambda b, j: (b, jnp.minimum((j + 1) * nb, n_seq_blocks - 1), 0))
    return pl.pallas_call(
        functools.partial(_mix_kernel, n_seq_blocks),
        grid=(B, L // T),
        in_specs=[
            pl.BlockSpec(memory_space=pltpu.SMEM),
            row(D), row(D_Q),
            row(4 * LANES), prev, nxt,
            row(4 * LANES), prev, nxt,
            row(D_GMLP), row(2 * D_GMLP),
            _const_spec((N_GMLP_GROUPS // 2, CHUNK, 2 * CHUNK)),
            _const_spec((CHUNK, D_GMLP)),
            _const_spec((1, D_ATTN)),
            _const_spec((1, D_GMLP)),
            _const_spec((D_ATTN + D_GMLP, D)),
            _const_spec((1, D)),
        ],
        out_specs=row(D),
        out_shape=jax.ShapeDtypeStruct((B, L, D), F32),
        scratch_shapes=[
            pltpu.VMEM((T + 2 * BLOCK, 4 * LANES), BF16),
            pltpu.VMEM((T + 2 * BLOCK, 4 * LANES), BF16),
            pltpu.VMEM((T, D_ATTN + D_GMLP), BF16),
        ],
        compiler_params=pltpu.CompilerParams(
            dimension_semantics=("arbitrary", "arbitrary"),
            vmem_limit_bytes=VMEM_LIMIT),
        name="mix",
    )(sink, x, q, k4, k4, k4, v4, v4, v4, u, vg2, wsp, bs_tab, g_attn, g_gmlp, w_o, g_post)


def _ffn_kernel(x_ref, xp_ref, xn_ref, gpre_ref, w_ref, cg_ref, cu_ref,
                wd_ref, gpost_ref, o_ref, hbuf, fbuf):
    T = x_ref.shape[0]
    H = BF16_ROWS
    j = pl.program_id(1)
    g = gpre_ref[...]

    hbuf[H:H + T] = _rms(x_ref[...], g).astype(BF16)
    hp = _rms(xp_ref[...], g)
    hbuf[0:H] = jnp.where(j > 0, hp, jnp.zeros_like(hp)).astype(BF16)
    hn = _rms(xn_ref[...], g)
    hbuf[H + T:] = jnp.where(j < pl.num_programs(1) - 1, hn, jnp.zeros_like(hn)).astype(BF16)

    def conv(t, c):
        before = pltpu.roll(t, 1, 0)
        after = pltpu.roll(t, T + 2 * H - 1, 0)
        r = before * c[0:1] + t * c[1:2] + after * c[2:3] + c[3:4]
        return r[H:H + T]

    for c in range(N_FF_CHUNKS):
        cols = slice(c * FF_CHUNK, (c + 1) * FF_CHUNK)
        up_cols = slice(D_FF + c * FF_CHUNK, D_FF + (c + 1) * FF_CHUNK)
        a = conv(jnp.dot(hbuf[...], w_ref[:, cols], preferred_element_type=F32), cg_ref[c])
        b = conv(jnp.dot(hbuf[...], w_ref[:, up_cols], preferred_element_type=F32), cu_ref[c])
        fbuf[:, c * FF_CHUNK:(c + 1) * FF_CHUNK] = (a * jax.nn.sigmoid(a) * b).astype(BF16)
    acc = jnp.dot(fbuf[...], wd_ref[...], preferred_element_type=F32)
    o_ref[...] = x_ref[...] + _rms(acc, gpost_ref[...])


def _ffn_call(x, g_pre, w_in, cg, cu, wd, g_post):
    B, L, D = x.shape
    T = FFN_TILE
    H = BF16_ROWS
    per = T // H
    n_halo = L // H
    row = pl.BlockSpec((None, T, D), lambda b, j: (b, j, 0))
    prev = pl.BlockSpec((None, H, D), lambda b, j: (b, jnp.maximum(j * per - 1, 0), 0))
    nxt = pl.BlockSpec((None, H, D), lambda b, j: (b, jnp.minimum((j + 1) * per, n_halo - 1), 0))
    return pl.pallas_call(
        _ffn_kernel,
        grid=(B, L // T),
        in_specs=[
            row, prev, nxt,
            _const_spec((1, D)),
            _const_spec((D, 2 * D_FF)),
            _const_spec((N_FF_CHUNKS, 4, FF_CHUNK)),
            _const_spec((N_FF_CHUNKS, 4, FF_CHUNK)),
            _const_spec((D_FF, D)),
            _const_spec((1, D)),
        ],
        out_specs=row,
        out_shape=jax.ShapeDtypeStruct((B, L, D), F32),
        scratch_shapes=[pltpu.VMEM((T + 2 * H, D), BF16), pltpu.VMEM((T, D_FF), BF16)],
        compiler_params=pltpu.CompilerParams(
            dimension_semantics=("arbitrary", "arbitrary"),
            vmem_limit_bytes=VMEM_LIMIT),
        name="ffn",
    )(x, x, x, g_pre, w_in, cg, cu, wd, g_post)


def _rope_table(L):
    half = ROT_DIM // 2
    inv_freq = ROPE_THETA ** (-jnp.arange(0, ROT_DIM, 2, dtype=F32) / ROT_DIM)
    ang = jnp.arange(L, dtype=F32)[:, None] * inv_freq[None, :]
    lane = np.arange(LANES) % HEAD_DIM
    idx = lane % half
    cos = jnp.cos(ang)[:, idx]
    sin = jnp.sin(ang)[:, idx]
    cos_t = jnp.where(lane < ROT_DIM, cos, 1.0)
    sin_up = jnp.where((lane >= half) & (lane < ROT_DIM), sin, 0.0)
    sin_dn = jnp.where(lane < half, -sin, 0.0)
    return jnp.stack([cos_t, sin_up, sin_dn]).astype(F32)


def _chunked_conv(w, b):
    t = jnp.concatenate([w, b[None, :]], axis=0)
    return t.reshape(4, N_FF_CHUNKS, FF_CHUNK).transpose(1, 0, 2)


def _layer(x, prm):
    L = x.shape[1]
    q, k4, v4, u, vg2 = _proj_call(x, prm["g_pre"], prm["w_in"], _rope_table(L),
                                   prm["ln_g"], prm["ln_b"])
    x = _mix_call(x, q, k4, v4, u, vg2, prm["sink"], prm["wsp"], prm["bs_tab"],
                  prm["g_attn"], prm["g_gmlp"], prm["w_o"], prm["g_mix_post"])
    return _ffn_call(x, prm["g_ffn_pre"], prm["w_ffn_in"], prm["cg"], prm["cu"],
                     prm["wd"], prm["g_ffn_post"])


def kernel(x_prompt, x_sample, norm_mix_pre, w_in, attn_sink, gmlp_ln_g, gmlp_ln_b, gmlp_w_s, gmlp_b_s, out_norm_attn, out_norm_gmlp, w_o, norm_mix_post, norm_ffn_pre, w_ffn_in, conv_w, conv_b, w_ffn_out, norm_ffn_post):
    depth = w_in.shape[0]
    for l in range(depth):
        ws = gmlp_w_s[l]
        wsp = jnp.concatenate([ws[0::2], ws[1::2]], axis=-1).astype(BF16)
        bs_tab = jnp.repeat(gmlp_b_s[l].T, D_GMLP // N_GMLP_GROUPS, axis=1)
        prm = dict(
            g_pre=norm_mix_pre[l][None, :],
            w_in=w_in[l].astype(BF16),
            ln_g=gmlp_ln_g[l][None, :], ln_b=gmlp_ln_b[l][None, :],
            sink=attn_sink[l],
            wsp=wsp, bs_tab=bs_tab,
            g_attn=out_norm_attn[l][None, :], g_gmlp=out_norm_gmlp[l][None, :],
            w_o=w_o[l].astype(BF16),
            g_mix_post=norm_mix_post[l][None, :],
            g_ffn_pre=norm_ffn_pre[l][None, :],
            w_ffn_in=w_ffn_in[l].astype(BF16),
            cg=_chunked_conv(conv_w[l][:, :D_FF], conv_b[l][:D_FF]),
            cu=_chunked_conv(conv_w[l][:, D_FF:], conv_b[l][D_FF:]),
            wd=w_ffn_out[l].astype(BF16),
            g_ffn_post=norm_ffn_post[l][None, :],
        )
        x_prompt = _layer(x_prompt, prm)
        x_sample = _layer(x_sample, prm)
    return (x_prompt, x_sample)
```

```python
import numpy as np
import jax
import jax.numpy as jnp
from jax import lax
from jax.experimental import pallas as pl
from jax.experimental.pallas import tpu as pltpu

D_MODEL = 1024
D_ATTN = 512
D_GMLP = 512
HEAD_DIM = 64
N_Q_HEADS = 8
N_KV_HEADS = 2
ROT_DIM = 16
ROPE_THETA = 500000.0
WINDOW = 128
BLOCK = 128
N_GMLP_GROUPS = 8
CHUNK = 128
D_FF = 2816
EPS = 1e-6
D_Q = N_Q_HEADS * HEAD_DIM
D_KV = N_KV_HEADS * HEAD_DIM
D_IN_PROJ = D_Q + 2 * D_KV + 2 * D_GMLP

LANES = 128
BF16_ROWS = 16
FF_CHUNK = 256
N_FF_CHUNKS = D_FF // FF_CHUNK
VMEM_LIMIT = 56 * 1024 * 1024

PROJ_TILE = 1024
MIX_TILE = 1024
FFN_TILE = 1024

F32 = jnp.float32
BF16 = jnp.bfloat16
SQRT_HALF = float(np.sqrt(0.5))
LOG2E = float(np.log2(np.e))
NEG = float(np.finfo(np.float32).min)


def _rms(x, g):
    ms = jnp.mean(x * x, axis=-1, keepdims=True)
    return x * lax.rsqrt(ms + EPS) * g


def _gelu(x):
    return 0.5 * x * (1.0 + lax.erf(x * SQRT_HALF))


def _proj_kernel(x_ref, g_ref, w_ref, rope_ref, lng_ref, lnb_ref,
                 q_ref, k4_ref, v4_ref, u_ref, vg2_ref, hbuf):
    hbuf[...] = _rms(x_ref[...], g_ref[...]).astype(BF16)

    def project(lo, hi):
        return jnp.dot(hbuf[...], w_ref[:, lo:hi], preferred_element_type=F32)

    cos = rope_ref[0]
    sin_up = rope_ref[1]
    sin_dn = rope_ref[2]
    lane = lax.broadcasted_iota(jnp.int32, (1, LANES), 1)
    first_head = lane < HEAD_DIM

    def rope(t):
        return (t * cos + pltpu.roll(t, ROT_DIM // 2, 1) * sin_up
                + pltpu.roll(t, LANES - ROT_DIM // 2, 1) * sin_dn)

    def spread(t, out_ref):
        sw = pltpu.roll(t, HEAD_DIM, 1)
        zero = jnp.zeros_like(t)
        parts = (jnp.where(first_head, t, zero), jnp.where(first_head, zero, sw),
                 jnp.where(first_head, sw, zero), jnp.where(first_head, zero, t))
        for i, part in enumerate(parts):
            out_ref[:, i * LANES:(i + 1) * LANES] = part.astype(BF16)

    def finish_gate(p):
        gv = _gelu(p)
        mu = jnp.mean(gv, axis=-1, keepdims=True)
        cen = gv - mu
        var = jnp.mean(cen * cen, axis=-1, keepdims=True)
        vn = cen * lax.rsqrt(var + EPS) * lng_ref[...] + lnb_ref[...]
        for i in range(D_GMLP // LANES):
            t = vn[:, i * LANES:(i + 1) * LANES]
            zero = jnp.zeros_like(t)
            vg2_ref[:, i * LANES:(i + 1) * LANES] = jnp.where(first_head, t, zero).astype(BF16)
            vg2_ref[:, D_GMLP + i * LANES:D_GMLP + (i + 1) * LANES] = (
                jnp.where(first_head, zero, t).astype(BF16))

    def finish_kv(p):
        spread(rope(p[:, 0:D_KV]), k4_ref)
        spread(p[:, D_KV:], v4_ref)

    def finish_q(p):
        scale = HEAD_DIM ** -0.5 * LOG2E
        for i in range(D_Q // LANES):
            sl = slice(i * LANES, (i + 1) * LANES)
            q_ref[:, sl] = (rope(p[:, sl]) * scale).astype(BF16)

    o3 = D_Q + 2 * D_KV
    p_q = project(0, D_Q)
    p_kv = project(D_Q, o3)
    p_gate = project(o3 + D_GMLP, D_IN_PROJ)
    finish_q(p_q)
    finish_kv(p_kv)
    p_u = project(o3, o3 + D_GMLP)
    finish_gate(p_gate)
    u_ref[...] = _gelu(p_u).astype(BF16)


def _const_spec(shape):
    nd = len(shape)
    return pl.BlockSpec(shape, lambda b, j: (0,) * nd, pipeline_mode=pl.Buffered(1))


def _proj_call(x, g_pre, w_in, rope_tab, ln_g, ln_b):
    B, L, D = x.shape
    T = PROJ_TILE
    row = lambda width: pl.BlockSpec((None, T, width), lambda b, j: (b, j, 0))
    out_widths = (D_Q, 4 * LANES, 4 * LANES, D_GMLP, 2 * D_GMLP)
    return pl.pallas_call(
        _proj_kernel,
        grid=(B, L // T),
        in_specs=[
            row(D),
            _const_spec((1, D)),
            _const_spec((D, D_IN_PROJ)),
            pl.BlockSpec((3, T, LANES), lambda b, j: (0, j, 0)),
            _const_spec((1, D_GMLP)),
            _const_spec((1, D_GMLP)),
        ],
        out_specs=[row(w) for w in out_widths],
        out_shape=[jax.ShapeDtypeStruct((B, L, w), BF16) for w in out_widths],
        scratch_shapes=[pltpu.VMEM((T, D), BF16)],
        compiler_params=pltpu.CompilerParams(
            dimension_semantics=("arbitrary", "arbitrary"),
            vmem_limit_bytes=VMEM_LIMIT),
        name="proj",
    )(x, g_pre, w_in, rope_tab, ln_g, ln_b)


def _mix_kernel(sink_ref, x_ref, q_ref, k_ref, kp_ref, kn_ref,
                v_ref, vp_ref, vn_ref, u_ref, vg2_ref, wsp_ref, bs_ref,
                ga_ref, gg_ref, wo_ref, gpost_ref, o_ref, kbuf, vbuf, mixed):
    T = x_ref.shape[0]
    nb = T // BLOCK
    j = pl.program_id(1)

    kbuf[0:BLOCK] = kp_ref[...]
    kbuf[BLOCK:BLOCK + T] = k_ref[...]
    kbuf[BLOCK + T:] = kn_ref[...]
    vbuf[0:BLOCK] = vp_ref[...]
    vbuf[BLOCK:BLOCK + T] = v_ref[...]
    vbuf[BLOCK + T:] = vn_ref[...]

    qi = lax.broadcasted_iota(jnp.int32, (2 * BLOCK, BLOCK), 0) % BLOCK
    ki = lax.broadcasted_iota(jnp.int32, (2 * BLOCK, BLOCK), 1)
    neg = jnp.full((2 * BLOCK, BLOCK), NEG, F32)
    zero = jnp.zeros((2 * BLOCK, BLOCK), F32)
    bias_prev = jnp.where(ki >= qi, zero, neg)
    bias_next = jnp.where(ki <= qi, zero, neg)
    top_rows = lax.broadcasted_iota(jnp.int32, (2 * BLOCK, 1), 0) < BLOCK
    nt = (((1,), (1,)), ((), ()))

    def biases(n):
        bias_p, bias_n = bias_prev, bias_next
        if n == 0:
            bias_p = jnp.where(j > 0, bias_prev, neg)
        if n == nb - 1:
            bias_n = jnp.where(j < pl.num_programs(1) - 1, bias_next, neg)
        return bias_p, bias_n

    def scores(n, g, half):
        rows = slice(n * BLOCK, (n + 1) * BLOCK)
        keys = slice(n * BLOCK, (n + 3) * BLOCK)
        c0 = 2 * g * LANES
        lhs = jnp.concatenate(
            [q_ref[rows, c0:c0 + LANES], q_ref[rows, c0 + LANES:c0 + 2 * LANES]], axis=0)
        cs = slice((2 * g + half) * LANES, (2 * g + half + 1) * LANES)
        return lax.dot_general(lhs, kbuf[keys, cs], nt, preferred_element_type=F32)

    def softmax(s, g, half, bias_p, bias_n):
        s0 = s[:, 0:BLOCK] + bias_p
        s1 = s[:, BLOCK:2 * BLOCK]
        s2 = s[:, 2 * BLOCK:] + bias_n
        sink = jnp.where(top_rows, sink_ref[4 * g + half], sink_ref[4 * g + 2 + half]) * LOG2E
        m = jnp.max(jnp.maximum(jnp.maximum(s0, s1), s2), axis=-1, keepdims=True)
        m = jnp.maximum(m, sink)
        e0 = jnp.exp2(s0 - m)
        e1 = jnp.exp2(s1 - m)
        e2 = jnp.exp2(s2 - m)
        den = jnp.sum(e0 + e1 + e2, axis=-1, keepdims=True) + jnp.exp2(sink - m)
        return jnp.concatenate([e0, e1, e2], axis=-1).astype(BF16), 1.0 / den

    def weighted_values(e, rden, n, g, half):
        keys = slice(n * BLOCK, (n + 3) * BLOCK)
        cs = slice((2 * g + half) * LANES, (2 * g + half + 1) * LANES)
        return jnp.dot(e, vbuf[keys, cs], preferred_element_type=F32) * rden

    sg_of_block = {}

    def spatial_gate(n0):
        halves = ([], [])
        for pair in range(N_GMLP_GROUPS // 2):
            ps = slice(pair * LANES, (pair + 1) * LANES)
            ps_hi = slice(D_GMLP + pair * LANES, D_GMLP + (pair + 1) * LANES)
            rhs = jnp.concatenate(
                [jnp.concatenate([vg2_ref[n * BLOCK:(n + 1) * BLOCK, ps],
                                  vg2_ref[n * BLOCK:(n + 1) * BLOCK, ps_hi]], axis=0)
                 for n in (n0, n0 + 1)], axis=1)
            both = jnp.dot(wsp_ref[pair], rhs, preferred_element_type=F32)
            halves[0].append(both[:, 0:LANES])
            halves[1].append(both[:, LANES:])
        sg_of_block[n0] = jnp.concatenate(halves[0], axis=-1)
        sg_of_block[n0 + 1] = jnp.concatenate(halves[1], axis=-1)

    def finish_block(n, outs):
        rows = slice(n * BLOCK, (n + 1) * BLOCK)
        a0 = outs[(0, 0)] + outs[(0, 1)]
        a1 = outs[(1, 0)] + outs[(1, 1)]
        attn = jnp.concatenate([a0[0:BLOCK], a0[BLOCK:], a1[0:BLOCK], a1[BLOCK:]], axis=-1)
        mixed[rows, 0:D_ATTN] = _rms(attn, ga_ref[...]).astype(BF16)
        if n % 2 == 0:
            spatial_gate(n)
        sg = sg_of_block.pop(n) + bs_ref[...]
        gm = u_ref[rows, :].astype(F32) * sg
        mixed[rows, D_ATTN:] = _rms(gm, gg_ref[...]).astype(BF16)

    chains = [(n, g, half) for n in range(nb) for g in range(N_KV_HEADS) for half in range(2)]
    last_of_block = (N_KV_HEADS - 1, 1)
    outs = {}

    def retire(e, rden, n, g, half):
        outs[(g, half)] = weighted_values(e, rden, n, g, half)
        if (g, half) == last_of_block:
            finish_block(n, outs)
            outs.clear()

    next_scores = scores(*chains[0])
    waiting = None
    bias_pn = biases(0)
    for i, (n, g, half) in enumerate(chains):
        s = next_scores
        if i + 1 < len(chains):
            next_scores = scores(*chains[i + 1])
        e, rden = softmax(s, g, half, *bias_pn)
        if waiting is not None:
            retire(*waiting)
        waiting = (e, rden, n, g, half)
        if (g, half) == last_of_block and n + 1 < nb:
            bias_pn = biases(n + 1)
    retire(*waiting)

    mm = jnp.dot(mixed[...], wo_ref[...], preferred_element_type=F32)
    o_ref[...] = x_ref[...] + _rms(mm, gpost_ref[...])


def _mix_call(x, q, k4, v4, u, vg2, sink, wsp, bs_tab, g_attn, g_gmlp, w_o, g_post):
    B, L, D = x.shape
    T = MIX_TILE
    nb = T // BLOCK
    n_seq_blocks = L // BLOCK
    row = lambda width: pl.BlockSpec((None, T, width), lambda b, j: (b, j, 0))
    prev = pl.BlockSpec((None, BLOCK, 4 * LANES),
                        lambda b, j: (b, jnp.maximum(j * nb - 1, 0), 0))
    nxt = pl.BlockSpec((None, BLOCK, 4 * LANES),
                       lambda b, j: (b, jnp.minimum((j + 1) * nb, n_seq_blocks - 1), 0))
    return pl.pallas_call(
        _mix_kernel,
        grid=(B, L // T),
        in_specs=[
            pl.BlockSpec(memory_space=pltpu.SMEM),
            row(D), row(D_Q),
            row(4 * LANES), prev, nxt,
            row(4 * LANES), prev, nxt,
            row(D_GMLP), row(2 * D_GMLP),
            _const_spec((N_GMLP_GROUPS // 2, CHUNK, 2 * CHUNK)),
            _const_spec((CHUNK, D_GMLP)),
            _const_spec((1, D_ATTN)),
            _const_spec((1, D_GMLP)),
            _const_spec((D_ATTN + D_GMLP, D)),
            _const_spec((1, D)),
        ],
        out_specs=row(D),
        out_shape=jax.ShapeDtypeStruct((B, L, D), F32),
        scratch_shapes=[
            pltpu.VMEM((T + 2 * BLOCK, 4 * LANES), BF16),
            pltpu.VMEM((T + 2 * BLOCK, 4 * LANES), BF16),
            pltpu.VMEM((T, D_ATTN + D_GMLP), BF16),
        ],
        compiler_params=pltpu.CompilerParams(
            dimension_semantics=("arbitrary", "arbitrary"),
            vmem_limit_bytes=VMEM_LIMIT),
        name="mix",
    )(sink, x, q, k4, k4, k4, v4, v4, v4, u, vg2, wsp, bs_tab, g_attn, g_gmlp, w_o, g_post)


def _ffn_kernel(x_ref, xp_ref, xn_ref, gpre_ref, w_ref, cg_ref, cu_ref,
                wd_ref, gpost_ref, o_ref, hbuf, fbuf):
    T = x_ref.shape[0]
    H = BF16_ROWS
    j = pl.program_id(1)
    g = gpre_ref[...]

    hbuf[H:H + T] = _rms(x_ref[...], g).astype(BF16)
    hp = _rms(xp_ref[...], g)
    hbuf[0:H] = jnp.where(j > 0, hp, jnp.zeros_like(hp)).astype(BF16)
    hn = _rms(xn_ref[...], g)
    hbuf[H + T:] = jnp.where(j < pl.num_programs(1) - 1, hn, jnp.zeros_like(hn)).astype(BF16)

    def conv(t, c):
        before = pltpu.roll(t, 1, 0)
        after = pltpu.roll(t, T + 2 * H - 1, 0)
        r = before * c[0:1] + t * c[1:2] + after * c[2:3] + c[3:4]
        return r[H:H + T]

    for c in range(N_FF_CHUNKS):
        cols = slice(c * FF_CHUNK, (c + 1) * FF_CHUNK)
        up_cols = slice(D_FF + c * FF_CHUNK, D_FF + (c + 1) * FF_CHUNK)
        a = conv(jnp.dot(hbuf[...], w_ref[:, cols], preferred_element_type=F32), cg_ref[c])
        b = conv(jnp.dot(hbuf[...], w_ref[:, up_cols], preferred_element_type=F32), cu_ref[c])
        fbuf[:, c * FF_CHUNK:(c + 1) * FF_CHUNK] = (a * jax.nn.sigmoid(a) * b).astype(BF16)
    acc = jnp.dot(fbuf[...], wd_ref[...], preferred_element_type=F32)
    o_ref[...] = x_ref[...] + _rms(acc, gpost_ref[...])


def _ffn_call(x, g_pre, w_in, cg, cu, wd, g_post):
    B, L, D = x.shape
    T = FFN_TILE
    H = BF16_ROWS
    per = T // H
    n_halo = L // H
    row = pl.BlockSpec((None, T, D), lambda b, j: (b, j, 0))
    prev = pl.BlockSpec((None, H, D), lambda b, j: (b, jnp.maximum(j * per - 1, 0), 0))
    nxt = pl.BlockSpec((None, H, D), lambda b, j: (b, jnp.minimum((j + 1) * per, n_halo - 1), 0))
    return pl.pallas_call(
        _ffn_kernel,
        grid=(B, L // T),
        in_specs=[
            row, prev, nxt,
            _const_spec((1, D)),
            _const_spec((D, 2 * D_FF)),
            _const_spec((N_FF_CHUNKS, 4, FF_CHUNK)),
            _const_spec((N_FF_CHUNKS, 4, FF_CHUNK)),
            _const_spec((D_FF, D)),
            _const_spec((1, D)),
        ],
        out_specs=row,
        out_shape=jax.ShapeDtypeStruct((B, L, D), F32),
        scratch_shapes=[pltpu.VMEM((T + 2 * H, D), BF16), pltpu.VMEM((T, D_FF), BF16)],
        compiler_params=pltpu.CompilerParams(
            dimension_semantics=("arbitrary", "arbitrary"),
            vmem_limit_bytes=VMEM_LIMIT),
        name="ffn",
    )(x, x, x, g_pre, w_in, cg, cu, wd, g_post)


def _rope_table(L):
    half = ROT_DIM // 2
    inv_freq = ROPE_THETA ** (-jnp.arange(0, ROT_DIM, 2, dtype=F32) / ROT_DIM)
    ang = jnp.arange(L, dtype=F32)[:, None] * inv_freq[None, :]
    lane = np.arange(LANES) % HEAD_DIM
    idx = lane % half
    cos = jnp.cos(ang)[:, idx]
    sin = jnp.sin(ang)[:, idx]
    cos_t = jnp.where(lane < ROT_DIM, cos, 1.0)
    sin_up = jnp.where((lane >= half) & (lane < ROT_DIM), sin, 0.0)
    sin_dn = jnp.where(lane < half, -sin, 0.0)
    return jnp.stack([cos_t, sin_up, sin_dn]).astype(F32)


def _chunked_conv(w, b):
    t = jnp.concatenate([w, b[None, :]], axis=0)
    return t.reshape(4, N_FF_CHUNKS, FF_CHUNK).transpose(1, 0, 2)


def _layer(x, prm):
    L = x.shape[1]
    q, k4, v4, u, vg2 = _proj_call(x, prm["g_pre"], prm["w_in"], _rope_table(L),
                                   prm["ln_g"], prm["ln_b"])
    x = _mix_call(x, q, k4, v4, u, vg2, prm["sink"], prm["wsp"], prm["bs_tab"],
                  prm["g_attn"], prm["g_gmlp"], prm["w_o"], prm["g_mix_post"])
    return _ffn_call(x, prm["g_ffn_pre"], prm["w_ffn_in"], prm["cg"], prm["cu"],
                     prm["wd"], prm["g_ffn_post"])


def kernel(x_prompt, x_sample, norm_mix_pre, w_in, attn_sink, gmlp_ln_g, gmlp_ln_b, gmlp_w_s, gmlp_b_s, out_norm_attn, out_norm_gmlp, w_o, norm_mix_post, norm_ffn_pre, w_ffn_in, conv_w, conv_b, w_ffn_out, norm_ffn_post):
    depth = w_in.shape[0]
    for l in range(depth):
        ws = gmlp_w_s[l]
        wsp = jnp.concatenate([ws[0::2], ws[1::2]], axis=-1).astype(BF16)
        bs_tab = jnp.repeat(gmlp_b_s[l].T, D_GMLP // N_GMLP_GROUPS, axis=1)
        prm = dict(
            g_pre=norm_mix_pre[l][None, :],
            w_in=w_in[l].astype(BF16),
            ln_g=gmlp_ln_g[l][None, :], ln_b=gmlp_ln_b[l][None, :],
            sink=attn_sink[l],
            wsp=wsp, bs_tab=bs_tab,
            g_attn=out_norm_attn[l][None, :], g_gmlp=out_norm_gmlp[l][None, :],
            w_o=w_o[l].astype(BF16),
            g_mix_post=norm_mix_post[l][None, :],
            g_ffn_pre=norm_ffn_pre[l][None, :],
            w_ffn_in=w_ffn_in[l].astype(BF16),
            cg=_chunked_conv(conv_w[l][:, :D_FF], conv_b[l][:D_FF]),
            cu=_chunked_conv(conv_w[l][:, D_FF:], conv_b[l][D_FF:]),
            wd=w_ffn_out[l].astype(BF16),
            g_ffn_post=norm_ffn_post[l][None, :],
        )
        x_prompt = _layer(x_prompt, prm)
        x_sample = _layer(x_sample, prm)
    return (x_prompt, x_sample)
```

```python
import numpy as np
import jax
import jax.numpy as jnp
from jax import lax
from jax.experimental import pallas as pl
from jax.experimental.pallas import tpu as pltpu

D_MODEL = 1024
D_ATTN = 512
D_GMLP = 512
HEAD_DIM = 64
N_Q_HEADS = 8
N_KV_HEADS = 2
ROT_DIM = 16
ROPE_THETA = 500000.0
WINDOW = 128
BLOCK = 128
N_GMLP_GROUPS = 8
CHUNK = 128
D_FF = 2816
EPS = 1e-6
D_Q = N_Q_HEADS * HEAD_DIM
D_KV = N_KV_HEADS * HEAD_DIM
D_IN_PROJ = D_Q + 2 * D_KV + 2 * D_GMLP

LANES = 128
BF16_ROWS = 16
FF_CHUNK = 256
N_FF_CHUNKS = D_FF // FF_CHUNK
VMEM_LIMIT = 56 * 1024 * 1024

PROJ_TILE = 1024
MIX_TILE = 1024
FFN_TILE = 1024

F32 = jnp.float32
BF16 = jnp.bfloat16
SQRT_HALF = float(np.sqrt(0.5))
LOG2E = float(np.log2(np.e))
NEG = float(np.finfo(np.float32).min)


def _rms(x, g):
    ms = jnp.mean(x * x, axis=-1, keepdims=True)
    return x * lax.rsqrt(ms + EPS) * g


def _gelu(x):
    return 0.5 * x * (1.0 + lax.erf(x * SQRT_HALF))


def _proj_kernel(x_ref, g_ref, w_ref, rope_ref, lng_ref, lnb_ref,
                 q_ref, k4_ref, v4_ref, u_ref, vg2_ref, hbuf):
    hbuf[...] = _rms(x_ref[...], g_ref[...]).astype(BF16)

    def project(lo, hi):
        return jnp.dot(hbuf[...], w_ref[:, lo:hi], preferred_element_type=F32)

    cos = rope_ref[0]
    sin_up = rope_ref[1]
    sin_dn = rope_ref[2]
    lane = lax.broadcasted_iota(jnp.int32, (1, LANES), 1)
    first_head = lane < HEAD_DIM

    def rope(t):
        return (t * cos + pltpu.roll(t, ROT_DIM // 2, 1) * sin_up
                + pltpu.roll(t, LANES - ROT_DIM // 2, 1) * sin_dn)

    def spread(t, out_ref):
        sw = pltpu.roll(t, HEAD_DIM, 1)
        zero = jnp.zeros_like(t)
        parts = (jnp.where(first_head, t, zero), jnp.where(first_head, zero, sw),
                 jnp.where(first_head, sw, zero), jnp.where(first_head, zero, t))
        for i, part in enumerate(parts):
            out_ref[:, i * LANES:(i + 1) * LANES] = part.astype(BF16)

    def finish_gate(p):
        gv = _gelu(p)
        mu = jnp.mean(gv, axis=-1, keepdims=True)
        cen = gv - mu
        var = jnp.mean(cen * cen, axis=-1, keepdims=True)
        vn = cen * lax.rsqrt(var + EPS) * lng_ref[...] + lnb_ref[...]
        for i in range(D_GMLP // LANES):
            t = vn[:, i * LANES:(i + 1) * LANES]
            zero = jnp.zeros_like(t)
            vg2_ref[:, i * LANES:(i + 1) * LANES] = jnp.where(first_head, t, zero).astype(BF16)
            vg2_ref[:, D_GMLP + i * LANES:D_GMLP + (i + 1) * LANES] = (
                jnp.where(first_head, zero, t).astype(BF16))

    def finish_kv(p):
        spread(rope(p[:, 0:D_KV]), k4_ref)
        spread(p[:, D_KV:], v4_ref)

    def finish_q(p):
        scale = HEAD_DIM ** -0.5 * LOG2E
        for i in range(D_Q // LANES):
            sl = slice(i * LANES, (i + 1) * LANES)
            q_ref[:, sl] = (rope(p[:, sl]) * scale).astype(BF16)

    o3 = D_Q + 2 * D_KV
    p_q = project(0, D_Q)
    p_kv = project(D_Q, o3)
    p_gate = project(o3 + D_GMLP, D_IN_PROJ)
    finish_q(p_q)
    finish_kv(p_kv)
    p_u = project(o3, o3 + D_GMLP)
    finish_gate(p_gate)
    u_ref[...] = _gelu(p_u).astype(BF16)


def _const_spec(shape):
    nd = len(shape)
    return pl.BlockSpec(shape, lambda b, j: (0,) * nd, pipeline_mode=pl.Buffered(1))


def _proj_call(x, g_pre, w_in, rope_tab, ln_g, ln_b):
    B, L, D = x.shape
    T = PROJ_TILE
    row = lambda width: pl.BlockSpec((None, T, width), lambda b, j: (b, j, 0))
    out_widths = (D_Q, 4 * LANES, 4 * LANES, D_GMLP, 2 * D_GMLP)
    return pl.pallas_call(
        _proj_kernel,
        grid=(B, L // T),
        in_specs=[
            row(D),
            _const_spec((1, D)),
            _const_spec((D, D_IN_PROJ)),
            pl.BlockSpec((3, T, LANES), lambda b, j: (0, j, 0)),
            _const_spec((1, D_GMLP)),
            _const_spec((1, D_GMLP)),
        ],
        out_specs=[row(w) for w in out_widths],
        out_shape=[jax.ShapeDtypeStruct((B, L, w), BF16) for w in out_widths],
        scratch_shapes=[pltpu.VMEM((T, D), BF16)],
        compiler_params=pltpu.CompilerParams(
            dimension_semantics=("arbitrary", "arbitrary"),
            vmem_limit_bytes=VMEM_LIMIT),
        name="proj",
    )(x, g_pre, w_in, rope_tab, ln_g, ln_b)


def _mix_kernel(sink_ref, x_ref, q_ref, k_ref, kp_ref, kn_ref,
                v_ref, vp_ref, vn_ref, u_ref, vg2_ref, wsp_ref, bs_ref,
                ga_ref, gg_ref, wo_ref, gpost_ref, o_ref, kbuf, vbuf, mixed):
    T = x_ref.shape[0]
    nb = T // BLOCK
    j = pl.program_id(1)

    kbuf[0:BLOCK] = kp_ref[...]
    kbuf[BLOCK:BLOCK + T] = k_ref[...]
    kbuf[BLOCK + T:] = kn_ref[...]
    vbuf[0:BLOCK] = vp_ref[...]
    vbuf[BLOCK:BLOCK + T] = v_ref[...]
    vbuf[BLOCK + T:] = vn_ref[...]

    qi = lax.broadcasted_iota(jnp.int32, (2 * BLOCK, BLOCK), 0) % BLOCK
    ki = lax.broadcasted_iota(jnp.int32, (2 * BLOCK, BLOCK), 1)
    neg = jnp.full((2 * BLOCK, BLOCK), NEG, F32)
    zero = jnp.zeros((2 * BLOCK, BLOCK), F32)
    bias_prev = jnp.where(ki >= qi, zero, neg)
    bias_next = jnp.where(ki <= qi, zero, neg)
    top_rows = lax.broadcasted_iota(jnp.int32, (2 * BLOCK, 1), 0) < BLOCK
    nt = (((1,), (1,)), ((), ()))

    def biases(n):
        bias_p, bias_n = bias_prev, bias_next
        if n == 0:
            bias_p = jnp.where(j > 0, bias_prev, neg)
        if n == nb - 1:
            bias_n = jnp.where(j < pl.num_programs(1) - 1, bias_next, neg)
        return bias_p, bias_n

    def scores(n, g, half):
        rows = slice(n * BLOCK, (n + 1) * BLOCK)
        keys = slice(n * BLOCK, (n + 3) * BLOCK)
        c0 = 2 * g * LANES
        lhs = jnp.concatenate(
            [q_ref[rows, c0:c0 + LANES], q_ref[rows, c0 + LANES:c0 + 2 * LANES]], axis=0)
        cs = slice((2 * g + half) * LANES, (2 * g + half + 1) * LANES)
        return lax.dot_general(lhs, kbuf[keys, cs], nt, preferred_element_type=F32)

    def softmax(s, g, half, bias_p, bias_n):
        s0 = s[:, 0:BLOCK] + bias_p
        s1 = s[:, BLOCK:2 * BLOCK]
        s2 = s[:, 2 * BLOCK:] + bias_n
        sink = jnp.where(top_rows, sink_ref[4 * g + half], sink_ref[4 * g + 2 + half]) * LOG2E
        m = jnp.max(jnp.maximum(jnp.maximum(s0, s1), s2), axis=-1, keepdims=True)
        m = jnp.maximum(m, sink)
        e0 = jnp.exp2(s0 - m)
        e1 = jnp.exp2(s1 - m)
        e2 = jnp.exp2(s2 - m)
        den = jnp.sum(e0 + e1 + e2, axis=-1, keepdims=True) + jnp.exp2(sink - m)
        return jnp.concatenate([e0, e1, e2], axis=-1).astype(BF16), 1.0 / den

    def weighted_values(e, rden, n, g, half):
        keys = slice(n * BLOCK, (n + 3) * BLOCK)
        cs = slice((2 * g + half) * LANES, (2 * g + half + 1) * LANES)
        return jnp.dot(e, vbuf[keys, cs], preferred_element_type=F32) * rden

    sg_of_block = {}

    def spatial_gate(n0):
        halves = ([], [])
        for pair in range(N_GMLP_GROUPS // 2):
            ps = slice(pair * LANES, (pair + 1) * LANES)
            ps_hi = slice(D_GMLP + pair * LANES, D_GMLP + (pair + 1) * LANES)
            rhs = jnp.concatenate(
                [jnp.concatenate([vg2_ref[n * BLOCK:(n + 1) * BLOCK, ps],
                                  vg2_ref[n * BLOCK:(n + 1) * BLOCK, ps_hi]], axis=0)
                 for n in (n0, n0 + 1)], axis=1)
            both = jnp.dot(wsp_ref[pair], rhs, preferred_element_type=F32)
            halves[0].append(both[:, 0:LANES])
            halves[1].append(both[:, LANES:])
        sg_of_block[n0] = jnp.concatenate(halves[0], axis=-1)
        sg_of_block[n0 + 1] = jnp.concatenate(halves[1], axis=-1)

    def finish_block(n, outs):
        rows = slice(n * BLOCK, (n + 1) * BLOCK)
        a0 = outs[(0, 0)] + outs[(0, 1)]
        a1 = outs[(1, 0)] + outs[(1, 1)]
        attn = jnp.concatenate([a0[0:BLOCK], a0[BLOCK:], a1[0:BLOCK], a1[BLOCK:]], axis=-1)
        mixed[rows, 0:D_ATTN] = _rms(attn, ga_ref[...]).astype(BF16)
        if n % 2 == 0:
            spatial_gate(n)
        sg = sg_of_block.pop(n) + bs_ref[...]
        gm = u_ref[rows, :].astype(F32) * sg
        mixed[rows, D_ATTN:] = _rms(gm, gg_ref[...]).astype(BF16)

    chains = [(n, g, half) for n in range(nb) for g in range(N_KV_HEADS) for half in range(2)]
    last_of_block = (N_KV_HEADS - 1, 1)
    outs = {}

    def retire(e, rden, n, g, half):
        outs[(g, half)] = weighted_values(e, rden, n, g, half)
        if (g, half) == last_of_block:
            finish_block(n, outs)
            outs.clear()

    next_scores = scores(*chains[0])
    waiting = None
    bias_pn = biases(0)
    for i, (n, g, half) in enumerate(chains):
        s = next_scores
        if i + 1 < len(chains):
            next_scores = scores(*chains[i + 1])
        e, rden = softmax(s, g, half, *bias_pn)
        if waiting is not None:
            retire(*waiting)
        waiting = (e, rden, n, g, half)
        if (g, half) == last_of_block and n + 1 < nb:
            bias_pn = biases(n + 1)
    retire(*waiting)

    for h in range(2):
        half_rows = slice(h * (T // 2), (h + 1) * (T // 2))
        mm = jnp.dot(mixed[half_rows, :], wo_ref[...], preferred_element_type=F32)
        o_ref[half_rows, :] = x_ref[half_rows, :] + _rms(mm, gpost_ref[...])


def _mix_call(x, q, k4, v4, u, vg2, sink, wsp, bs_tab, g_attn, g_gmlp, w_o, g_post):
    B, L, D = x.shape
    T = MIX_TILE
    nb = T // BLOCK
    n_seq_blocks = L // BLOCK
    row = lambda width: pl.BlockSpec((None, T, width), lambda b, j: (b, j, 0))
    prev = pl.BlockSpec((None, BLOCK, 4 * LANES),
                        lambda b, j: (b, jnp.maximum(j * nb - 1, 0), 0))
    nxt = pl.BlockSpec((None, BLOCK, 4 * LANES),
                       lambda b, j: (b, jnp.minimum((j + 1) * nb, n_seq_blocks - 1), 0))
    return pl.pallas_call(
        _mix_kernel,
        grid=(B, L // T),
        in_specs=[
            pl.BlockSpec(memory_space=pltpu.SMEM),
            row(D), row(D_Q),
            row(4 * LANES), prev, nxt,
            row(4 * LANES), prev, nxt,
            row(D_GMLP), row(2 * D_GMLP),
            _const_spec((N_GMLP_GROUPS // 2, CHUNK, 2 * CHUNK)),
            _const_spec((CHUNK, D_GMLP)),
            _const_spec((1, D_ATTN)),
            _const_spec((1, D_GMLP)),
            _const_spec((D_ATTN + D_GMLP, D)),
            _const_spec((1, D)),
        ],
        out_specs=row(D),
        out_shape=jax.ShapeDtypeStruct((B, L, D), F32),
        scratch_shapes=[
            pltpu.VMEM((T + 2 * BLOCK, 4 * LANES), BF16),
            pltpu.VMEM((T + 2 * BLOCK, 4 * LANES), BF16),
            pltpu.VMEM((T, D_ATTN + D_GMLP), BF16),
        ],
        compiler_params=pltpu.CompilerParams(
            dimension_semantics=("arbitrary", "arbitrary"),
            vmem_limit_bytes=VMEM_LIMIT),
        name="mix",
    )(sink, x, q, k4, k4, k4, v4, v4, v4, u, vg2, wsp, bs_tab, g_attn, g_gmlp, w_o, g_post)


def _ffn_kernel(x_ref, xp_ref, xn_ref, gpre_ref, w_ref, cg_ref, cu_ref,
                wd_ref, gpost_ref, o_ref, hbuf, fbuf):
    T = x_ref.shape[0]
    H = BF16_ROWS
    j = pl.program_id(1)
    g = gpre_ref[...]

    hbuf[H:H + T] = _rms(x_ref[...], g).astype(BF16)
    hp = _rms(xp_ref[...], g)
    hbuf[0:H] = jnp.where(j > 0, hp, jnp.zeros_like(hp)).astype(BF16)
    hn = _rms(xn_ref[...], g)
    hbuf[H + T:] = jnp.where(j < pl.num_programs(1) - 1, hn, jnp.zeros_like(hn)).astype(BF16)

    def conv(t, c):
        before = pltpu.roll(t, 1, 0)
        after = pltpu.roll(t, T + 2 * H - 1, 0)
        r = before * c[0:1] + t * c[1:2] + after * c[2:3] + c[3:4]
        return r[H:H + T]

    for c in range(N_FF_CHUNKS):
        cols = slice(c * FF_CHUNK, (c + 1) * FF_CHUNK)
        up_cols = slice(D_FF + c * FF_CHUNK, D_FF + (c + 1) * FF_CHUNK)
        a = conv(jnp.dot(hbuf[...], w_ref[:, cols], preferred_element_type=F32), cg_ref[c])
        b = conv(jnp.dot(hbuf[...], w_ref[:, up_cols], preferred_element_type=F32), cu_ref[c])
        fbuf[:, c * FF_CHUNK:(c + 1) * FF_CHUNK] = (a * jax.nn.sigmoid(a) * b).astype(BF16)
    acc = jnp.dot(fbuf[...], wd_ref[...], preferred_element_type=F32)
    o_ref[...] = x_ref[...] + _rms(acc, gpost_ref[...])


def _ffn_call(x, g_pre, w_in, cg, cu, wd, g_post):
    B, L, D = x.shape
    T = FFN_TILE
    H = BF16_ROWS
    per = T // H
    n_halo = L // H
    row = pl.BlockSpec((None, T, D), lambda b, j: (b, j, 0))
    prev = pl.BlockSpec((None, H, D), lambda b, j: (b, jnp.maximum(j * per - 1, 0), 0))
    nxt = pl.BlockSpec((None, H, D), lambda b, j: (b, jnp.minimum((j + 1) * per, n_halo - 1), 0))
    return pl.pallas_call(
        _ffn_kernel,
        grid=(B, L // T),
        in_specs=[
            row, prev, nxt,
            _const_spec((1, D)),
            _const_spec((D, 2 * D_FF)),
            _const_spec((N_FF_CHUNKS, 4, FF_CHUNK)),
            _const_spec((N_FF_CHUNKS, 4, FF_CHUNK)),
            _const_spec((D_FF, D)),
            _const_spec((1, D)),
        ],
        out_specs=row,
        out_shape=jax.ShapeDtypeStruct((B, L, D), F32),
        scratch_shapes=[pltpu.VMEM((T + 2 * H, D), BF16), pltpu.VMEM((T, D_FF), BF16)],
        compiler_params=pltpu.CompilerParams(
            dimension_semantics=("arbitrary", "arbitrary"),
            vmem_limit_bytes=VMEM_LIMIT),
        name="ffn",
    )(x, x, x, g_pre, w_in, cg, cu, wd, g_post)


def _rope_table(L):
    half = ROT_DIM // 2
    inv_freq = ROPE_THETA ** (-jnp.arange(0, ROT_DIM, 2, dtype=F32) / ROT_DIM)
    ang = jnp.arange(L, dtype=F32)[:, None] * inv_freq[None, :]
    lane = np.arange(LANES) % HEAD_DIM
    idx = lane % half
    cos = jnp.cos(ang)[:, idx]
    sin = jnp.sin(ang)[:, idx]
    cos_t = jnp.where(lane < ROT_DIM, cos, 1.0)
    sin_up = jnp.where((lane >= half) & (lane < ROT_DIM), sin, 0.0)
    sin_dn = jnp.where(lane < half, -sin, 0.0)
    return jnp.stack([cos_t, sin_up, sin_dn]).astype(F32)


def _chunked_conv(w, b):
    t = jnp.concatenate([w, b[None, :]], axis=0)
    return t.reshape(4, N_FF_CHUNKS, FF_CHUNK).transpose(1, 0, 2)


def _layer(x, prm):
    L = x.shape[1]
    q, k4, v4, u, vg2 = _proj_call(x, prm["g_pre"], prm["w_in"], _rope_table(L),
                                   prm["ln_g"], prm["ln_b"])
    x = _mix_call(x, q, k4, v4, u, vg2, prm["sink"], prm["wsp"], prm["bs_tab"],
                  prm["g_attn"], prm["g_gmlp"], prm["w_o"], prm["g_mix_post"])
    return _ffn_call(x, prm["g_ffn_pre"], prm["w_ffn_in"], prm["cg"], prm["cu"],
                     prm["wd"], prm["g_ffn_post"])


def kernel(x_prompt, x_sample, norm_mix_pre, w_in, attn_sink, gmlp_ln_g, gmlp_ln_b, gmlp_w_s, gmlp_b_s, out_norm_attn, out_norm_gmlp, w_o, norm_mix_post, norm_ffn_pre, w_ffn_in, conv_w, conv_b, w_ffn_out, norm_ffn_post):
    depth = w_in.shape[0]
    for l in range(depth):
        ws = gmlp_w_s[l]
        wsp = jnp.concatenate([ws[0::2], ws[1::2]], axis=-1).astype(BF16)
        bs_tab = jnp.repeat(gmlp_b_s[l].T, D_GMLP // N_GMLP_GROUPS, axis=1)
        prm = dict(
            g_pre=norm_mix_pre[l][None, :],
            w_in=w_in[l].astype(BF16),
            ln_g=gmlp_ln_g[l][None, :], ln_b=gmlp_ln_b[l][None, :],
            sink=attn_sink[l],
            wsp=wsp, bs_tab=bs_tab,
            g_attn=out_norm_attn[l][None, :], g_gmlp=out_norm_gmlp[l][None, :],
            w_o=w_o[l].astype(BF16),
            g_mix_post=norm_mix_post[l][None, :],
            g_ffn_pre=norm_ffn_pre[l][None, :],
            w_ffn_in=w_ffn_in[l].astype(BF16),
            cg=_chunked_conv(conv_w[l][:, :D_FF], conv_b[l][:D_FF]),
            cu=_chunked_conv(conv_w[l][:, D_FF:], conv_b[l][D_FF:]),
            wd=w_ffn_out[l].astype(BF16),
            g_ffn_post=norm_ffn_post[l][None, :],
        )
        x_prompt = _layer(x_prompt, prm)
        x_sample = _layer(x_sample, prm)
    return (x_prompt, x_sample)
```

```python
import numpy as np
import jax
import jax.numpy as jnp
from jax import lax
from jax.experimental import pallas as pl
from jax.experimental.pallas import tpu as pltpu

D_MODEL = 1024
D_ATTN = 512
D_GMLP = 512
HEAD_DIM = 64
N_Q_HEADS = 8
N_KV_HEADS = 2
ROT_DIM = 16
ROPE_THETA = 500000.0
WINDOW = 128
BLOCK = 128
N_GMLP_GROUPS = 8
CHUNK = 128
D_FF = 2816
EPS = 1e-6
D_Q = N_Q_HEADS * HEAD_DIM
D_KV = N_KV_HEADS * HEAD_DIM
D_IN_PROJ = D_Q + 2 * D_KV + 2 * D_GMLP

LANES = 128
BF16_ROWS = 16
FF_CHUNK = 256
N_FF_CHUNKS = D_FF // FF_CHUNK
VMEM_LIMIT = 56 * 1024 * 1024

PROJ_TILE = 1024
MIX_TILE = 1024
FFN_TILE = 1024

F32 = jnp.float32
BF16 = jnp.bfloat16
SQRT_HALF = float(np.sqrt(0.5))
LOG2E = float(np.log2(np.e))
NEG = float(np.finfo(np.float32).min)


def _rms(x, g):
    ms = jnp.mean(x * x, axis=-1, keepdims=True)
    return x * lax.rsqrt(ms + EPS) * g


def _gelu(x):
    return 0.5 * x * (1.0 + lax.erf(x * SQRT_HALF))


def _proj_kernel(x_ref, g_ref, w_ref, rope_ref, lng_ref, lnb_ref,
                 q_ref, k4_ref, v4_ref, u_ref, vg2_ref, hbuf):
    hbuf[...] = _rms(x_ref[...], g_ref[...]).astype(BF16)

    def project(lo, hi):
        return jnp.dot(hbuf[...], w_ref[:, lo:hi], preferred_element_type=F32)

    cos = rope_ref[0]
    sin_up = rope_ref[1]
    sin_dn = rope_ref[2]
    lane = lax.broadcasted_iota(jnp.int32, (1, LANES), 1)
    first_head = lane < HEAD_DIM

    def rope(t):
        return (t * cos + pltpu.roll(t, ROT_DIM // 2, 1) * sin_up
                + pltpu.roll(t, LANES - ROT_DIM // 2, 1) * sin_dn)

    def spread(t, out_ref):
        sw = pltpu.roll(t, HEAD_DIM, 1)
        zero = jnp.zeros_like(t)
        parts = (jnp.where(first_head, t, zero), jnp.where(first_head, zero, sw),
                 jnp.where(first_head, sw, zero), jnp.where(first_head, zero, t))
        for i, part in enumerate(parts):
            out_ref[:, i * LANES:(i + 1) * LANES] = part.astype(BF16)

    def finish_gate(p):
        gv = _gelu(p)
        mu = jnp.mean(gv, axis=-1, keepdims=True)
        cen = gv - mu
        var = jnp.mean(cen * cen, axis=-1, keepdims=True)
        vn = cen * lax.rsqrt(var + EPS) * lng_ref[...] + lnb_ref[...]
        for i in range(D_GMLP // LANES):
            t = vn[:, i * LANES:(i + 1) * LANES]
            zero = jnp.zeros_like(t)
            vg2_ref[:, i * LANES:(i + 1) * LANES] = jnp.where(first_head, t, zero).astype(BF16)
            vg2_ref[:, D_GMLP + i * LANES:D_GMLP + (i + 1) * LANES] = (
                jnp.where(first_head, zero, t).astype(BF16))

    def finish_kv(p):
        spread(rope(p[:, 0:D_KV]), k4_ref)
        spread(p[:, D_KV:], v4_ref)

    def finish_q(p):
        scale = HEAD_DIM ** -0.5 * LOG2E
        for i in range(D_Q // LANES):
            sl = slice(i * LANES, (i + 1) * LANES)
            q_ref[:, sl] = (rope(p[:, sl]) * scale).astype(BF16)

    o3 = D_Q + 2 * D_KV
    p_q = project(0, D_Q)
    p_kv = project(D_Q, o3)
    p_gate = project(o3 + D_GMLP, D_IN_PROJ)
    finish_q(p_q)
    finish_kv(p_kv)
    p_u = project(o3, o3 + D_GMLP)
    finish_gate(p_gate)
    u_ref[...] = _gelu(p_u).astype(BF16)


def _const_spec(shape):
    nd = len(shape)
    return pl.BlockSpec(shape, lambda b, j: (0,) * nd, pipeline_mode=pl.Buffered(1))


def _proj_call(x, g_pre, w_in, rope_tab, ln_g, ln_b):
    B, L, D = x.shape
    T = PROJ_TILE
    assert D == D_MODEL and L % T == 0 and rope_tab.shape[1] >= L, (x.shape, rope_tab.shape)
    assert w_in.shape == (D, D_IN_PROJ) and w_in.dtype == BF16, (w_in.shape, w_in.dtype)
    row = lambda width: pl.BlockSpec((None, T, width), lambda b, j: (b, j, 0))
    out_widths = (D_Q, 4 * LANES, 4 * LANES, D_GMLP, 2 * D_GMLP)
    return pl.pallas_call(
        _proj_kernel,
        grid=(B, L // T),
        in_specs=[
            row(D),
            _const_spec((1, D)),
            _const_spec((D, D_IN_PROJ)),
            pl.BlockSpec((3, T, LANES), lambda b, j: (0, j, 0)),
            _const_spec((1, D_GMLP)),
            _const_spec((1, D_GMLP)),
        ],
        out_specs=[row(w) for w in out_widths],
        out_shape=[jax.ShapeDtypeStruct((B, L, w), BF16) for w in out_widths],
        scratch_shapes=[pltpu.VMEM((T, D), BF16)],
        compiler_params=pltpu.CompilerParams(
            dimension_semantics=("arbitrary", "arbitrary"),
            vmem_limit_bytes=VMEM_LIMIT),
        name="proj",
    )(x, g_pre, w_in, rope_tab, ln_g, ln_b)


def _mix_kernel(sink_ref, x_ref, q_ref, k_ref, kp_ref, kn_ref,
                v_ref, vp_ref, vn_ref, u_ref, vg2_ref, wsp_ref, bs_ref,
                ga_ref, gg_ref, wo_ref, gpost_ref, o_ref, kbuf, vbuf, mixed):
    T = x_ref.shape[0]
    nb = T // BLOCK
    j = pl.program_id(1)

    kbuf[0:BLOCK] = kp_ref[...]
    kbuf[BLOCK:BLOCK + T] = k_ref[...]
    kbuf[BLOCK + T:] = kn_ref[...]
    vbuf[0:BLOCK] = vp_ref[...]
    vbuf[BLOCK:BLOCK + T] = v_ref[...]
    vbuf[BLOCK + T:] = vn_ref[...]

    qi = lax.broadcasted_iota(jnp.int32, (2 * BLOCK, BLOCK), 0) % BLOCK
    ki = lax.broadcasted_iota(jnp.int32, (2 * BLOCK, BLOCK), 1)
    neg = jnp.full((2 * BLOCK, BLOCK), NEG, F32)
    zero = jnp.zeros((2 * BLOCK, BLOCK), F32)
    bias_prev = jnp.where(ki >= qi, zero, neg)
    bias_next = jnp.where(ki <= qi, zero, neg)
    top_rows = lax.broadcasted_iota(jnp.int32, (2 * BLOCK, 1), 0) < BLOCK
    nt = (((1,), (1,)), ((), ()))

    def biases(n):
        bias_p, bias_n = bias_prev, bias_next
        if n == 0:
            bias_p = jnp.where(j > 0, bias_prev, neg)
        if n == nb - 1:
            bias_n = jnp.where(j < pl.num_programs(1) - 1, bias_next, neg)
        return bias_p, bias_n

    def scores(n, g, half):
        rows = slice(n * BLOCK, (n + 1) * BLOCK)
        keys = slice(n * BLOCK, (n + 3) * BLOCK)
        c0 = 2 * g * LANES
        lhs = jnp.concatenate(
            [q_ref[rows, c0:c0 + LANES], q_ref[rows, c0 + LANES:c0 + 2 * LANES]], axis=0)
        cs = slice((2 * g + half) * LANES, (2 * g + half + 1) * LANES)
        return lax.dot_general(lhs, kbuf[keys, cs], nt, preferred_element_type=F32)

    def softmax(s, g, half, bias_p, bias_n):
        s0 = s[:, 0:BLOCK] + bias_p
        s1 = s[:, BLOCK:2 * BLOCK]
        s2 = s[:, 2 * BLOCK:] + bias_n
        sink = jnp.where(top_rows, sink_ref[4 * g + half], sink_ref[4 * g + 2 + half]) * LOG2E
        m = jnp.max(jnp.maximum(jnp.maximum(s0, s1), s2), axis=-1, keepdims=True)
        m = jnp.maximum(m, sink)
        e0 = jnp.exp2(s0 - m)
        e1 = jnp.exp2(s1 - m)
        e2 = jnp.exp2(s2 - m)
        den = jnp.sum(e0 + e1 + e2, axis=-1, keepdims=True) + jnp.exp2(sink - m)
        return jnp.concatenate([e0, e1, e2], axis=-1).astype(BF16), 1.0 / den

    def weighted_values(e, rden, n, g, half):
        keys = slice(n * BLOCK, (n + 3) * BLOCK)
        cs = slice((2 * g + half) * LANES, (2 * g + half + 1) * LANES)
        return jnp.dot(e, vbuf[keys, cs], preferred_element_type=F32) * rden

    sg_of_block = {}

    def spatial_gate(n0):
        halves = ([], [])
        for pair in range(N_GMLP_GROUPS // 2):
            ps = slice(pair * LANES, (pair + 1) * LANES)
            ps_hi = slice(D_GMLP + pair * LANES, D_GMLP + (pair + 1) * LANES)
            rhs = jnp.concatenate(
                [jnp.concatenate([vg2_ref[n * BLOCK:(n + 1) * BLOCK, ps],
                                  vg2_ref[n * BLOCK:(n + 1) * BLOCK, ps_hi]], axis=0)
                 for n in (n0, n0 + 1)], axis=1)
            both = jnp.dot(wsp_ref[pair], rhs, preferred_element_type=F32)
            halves[0].append(both[:, 0:LANES])
            halves[1].append(both[:, LANES:])
        sg_of_block[n0] = jnp.concatenate(halves[0], axis=-1)
        sg_of_block[n0 + 1] = jnp.concatenate(halves[1], axis=-1)

    def finish_block(n, outs):
        rows = slice(n * BLOCK, (n + 1) * BLOCK)
        a0 = outs[(0, 0)] + outs[(0, 1)]
        a1 = outs[(1, 0)] + outs[(1, 1)]
        attn = jnp.concatenate([a0[0:BLOCK], a0[BLOCK:], a1[0:BLOCK], a1[BLOCK:]], axis=-1)
        mixed[rows, 0:D_ATTN] = _rms(attn, ga_ref[...]).astype(BF16)
        if n % 2 == 0:
            spatial_gate(n)
        sg = sg_of_block.pop(n) + bs_ref[...]
        gm = u_ref[rows, :].astype(F32) * sg
        mixed[rows, D_ATTN:] = _rms(gm, gg_ref[...]).astype(BF16)

    chains = [(n, g, half) for n in range(nb) for g in range(N_KV_HEADS) for half in range(2)]
    last_of_block = (N_KV_HEADS - 1, 1)
    outs = {}

    def retire(e, rden, n, g, half):
        outs[(g, half)] = weighted_values(e, rden, n, g, half)
        if (g, half) == last_of_block:
            finish_block(n, outs)
            outs.clear()

    next_scores = scores(*chains[0])
    waiting = None
    bias_pn = biases(0)
    for i, (n, g, half) in enumerate(chains):
        s = next_scores
        if i + 1 < len(chains):
            next_scores = scores(*chains[i + 1])
        e, rden = softmax(s, g, half, *bias_pn)
        if waiting is not None:
            retire(*waiting)
        waiting = (e, rden, n, g, half)
        if (g, half) == last_of_block and n + 1 < nb:
            bias_pn = biases(n + 1)
    retire(*waiting)

    for h in range(2):
        half_rows = slice(h * (T // 2), (h + 1) * (T // 2))
        mm = jnp.dot(mixed[half_rows, :], wo_ref[...], preferred_element_type=F32)
        o_ref[half_rows, :] = x_ref[half_rows, :] + _rms(mm, gpost_ref[...])


def _mix_call(x, q, k4, v4, u, vg2, sink, wsp, bs_tab, g_attn, g_gmlp, w_o, g_post):
    B, L, D = x.shape
    T = MIX_TILE
    nb = T // BLOCK
    n_seq_blocks = L // BLOCK
    assert D == D_MODEL and L % T == 0 and nb % 2 == 0, (x.shape, T)
    assert w_o.shape == (D_ATTN + D_GMLP, D) and w_o.dtype == BF16, (w_o.shape, w_o.dtype)
    row = lambda width: pl.BlockSpec((None, T, width), lambda b, j: (b, j, 0))
    prev = pl.BlockSpec((None, BLOCK, 4 * LANES),
                        lambda b, j: (b, jnp.maximum(j * nb - 1, 0), 0))
    nxt = pl.BlockSpec((None, BLOCK, 4 * LANES),
                       lambda b, j: (b, jnp.minimum((j + 1) * nb, n_seq_blocks - 1), 0))
    return pl.pallas_call(
        _mix_kernel,
        grid=(B, L // T),
        in_specs=[
            pl.BlockSpec(memory_space=pltpu.SMEM),
            row(D), row(D_Q),
            row(4 * LANES), prev, nxt,
            row(4 * LANES), prev, nxt,
            row(D_GMLP), row(2 * D_GMLP),
            _const_spec((N_GMLP_GROUPS // 2, CHUNK, 2 * CHUNK)),
            _const_spec((CHUNK, D_GMLP)),
            _const_spec((1, D_ATTN)),
            _const_spec((1, D_GMLP)),
            _const_spec((D_ATTN + D_GMLP, D)),
            _const_spec((1, D)),
        ],
        out_specs=row(D),
        out_shape=jax.ShapeDtypeStruct((B, L, D), F32),
        scratch_shapes=[
            pltpu.VMEM((T + 2 * BLOCK, 4 * LANES), BF16),
            pltpu.VMEM((T + 2 * BLOCK, 4 * LANES), BF16),
            pltpu.VMEM((T, D_ATTN + D_GMLP), BF16),
        ],
        compiler_params=pltpu.CompilerParams(
            dimension_semantics=("arbitrary", "arbitrary"),
            vmem_limit_bytes=VMEM_LIMIT),
        name="mix",
    )(sink, x, q, k4, k4, k4, v4, v4, v4, u, vg2, wsp, bs_tab, g_attn, g_gmlp, w_o, g_post)


def _ffn_kernel(x_ref, xp_ref, xn_ref, gpre_ref, w_ref, cg_ref, cu_ref,
                wd_ref, gpost_ref, o_ref, hbuf, fbuf):
    T = x_ref.shape[0]
    H = BF16_ROWS
    j = pl.program_id(1)
    g = gpre_ref[...]

    hbuf[H:H + T] = _rms(x_ref[...], g).astype(BF16)
    hp = _rms(xp_ref[...], g)
    hbuf[0:H] = jnp.where(j > 0, hp, jnp.zeros_like(hp)).astype(BF16)
    hn = _rms(xn_ref[...], g)
    hbuf[H + T:] = jnp.where(j < pl.num_programs(1) - 1, hn, jnp.zeros_like(hn)).astype(BF16)

    def conv(t, c):
        before = pltpu.roll(t, 1, 0)
        after = pltpu.roll(t, T + 2 * H - 1, 0)
        r = before * c[0:1] + t * c[1:2] + after * c[2:3] + c[3:4]
        return r[H:H + T]

    for c in range(N_FF_CHUNKS):
        cols = slice(c * FF_CHUNK, (c + 1) * FF_CHUNK)
        up_cols = slice(D_FF + c * FF_CHUNK, D_FF + (c + 1) * FF_CHUNK)
        a = conv(jnp.dot(hbuf[...], w_ref[:, cols], preferred_element_type=F32), cg_ref[c])
        b = conv(jnp.dot(hbuf[...], w_ref[:, up_cols], preferred_element_type=F32), cu_ref[c])
        fbuf[:, c * FF_CHUNK:(c + 1) * FF_CHUNK] = (a * jax.nn.sigmoid(a) * b).astype(BF16)
    acc = jnp.dot(fbuf[...], wd_ref[...], preferred_element_type=F32)
    o_ref[...] = x_ref[...] + _rms(acc, gpost_ref[...])


def _ffn_call(x, g_pre, w_in, cg, cu, wd, g_post):
    B, L, D = x.shape
    T = FFN_TILE
    H = BF16_ROWS
    per = T // H
    n_halo = L // H
    assert D == D_MODEL and L % T == 0 and T % H == 0, (x.shape, T)
    assert w_in.shape == (D, 2 * D_FF) and wd.shape == (D_FF, D), (w_in.shape, wd.shape)
    assert w_in.dtype == BF16 and wd.dtype == BF16, (w_in.dtype, wd.dtype)
    row = pl.BlockSpec((None, T, D), lambda b, j: (b, j, 0))
    prev = pl.BlockSpec((None, H, D), lambda b, j: (b, jnp.maximum(j * per - 1, 0), 0))
    nxt = pl.BlockSpec((None, H, D), lambda b, j: (b, jnp.minimum((j + 1) * per, n_halo - 1), 0))
    return pl.pallas_call(
        _ffn_kernel,
        grid=(B, L // T),
        in_specs=[
            row, prev, nxt,
            _const_spec((1, D)),
            _const_spec((D, 2 * D_FF)),
            _const_spec((N_FF_CHUNKS, 4, FF_CHUNK)),
            _const_spec((N_FF_CHUNKS, 4, FF_CHUNK)),
            _const_spec((D_FF, D)),
            _const_spec((1, D)),
        ],
        out_specs=row,
        out_shape=jax.ShapeDtypeStruct((B, L, D), F32),
        scratch_shapes=[pltpu.VMEM((T + 2 * H, D), BF16), pltpu.VMEM((T, D_FF), BF16)],
        compiler_params=pltpu.CompilerParams(
            dimension_semantics=("arbitrary", "arbitrary"),
            vmem_limit_bytes=VMEM_LIMIT),
        name="ffn",
    )(x, x, x, g_pre, w_in, cg, cu, wd, g_post)


def _rope_table(L):
    half = ROT_DIM // 2
    inv_freq = ROPE_THETA ** (-jnp.arange(0, ROT_DIM, 2, dtype=F32) / ROT_DIM)
    ang = jnp.arange(L, dtype=F32)[:, None] * inv_freq[None, :]
    lane = np.arange(LANES) % HEAD_DIM
    idx = lane % half
    cos = jnp.cos(ang)[:, idx]
    sin = jnp.sin(ang)[:, idx]
    cos_t = jnp.where(lane < ROT_DIM, cos, 1.0)
    sin_up = jnp.where((lane >= half) & (lane < ROT_DIM), sin, 0.0)
    sin_dn = jnp.where(lane < half, -sin, 0.0)
    return jnp.stack([cos_t, sin_up, sin_dn]).astype(F32)


def _chunked_conv(w, b):
    t = jnp.concatenate([w, b[None, :]], axis=0)
    return t.reshape(4, N_FF_CHUNKS, FF_CHUNK).transpose(1, 0, 2)


def _layer(x, prm):
    q, k4, v4, u, vg2 = _proj_call(x, prm["g_pre"], prm["w_in"], prm["rope"],
                                   prm["ln_g"], prm["ln_b"])
    x = _mix_call(x, q, k4, v4, u, vg2, prm["sink"], prm["wsp"], prm["bs_tab"],
                  prm["g_attn"], prm["g_gmlp"], prm["w_o"], prm["g_mix_post"])
    return _ffn_call(x, prm["g_ffn_pre"], prm["w_ffn_in"], prm["cg"], prm["cu"],
                     prm["wd"], prm["g_ffn_post"])


def kernel(x_prompt, x_sample, norm_mix_pre, w_in, attn_sink, gmlp_ln_g, gmlp_ln_b, gmlp_w_s, gmlp_b_s, out_norm_attn, out_norm_gmlp, w_o, norm_mix_post, norm_ffn_pre, w_ffn_in, conv_w, conv_b, w_ffn_out, norm_ffn_post):
    depth = w_in.shape[0]
    for l in range(depth):
        ws = gmlp_w_s[l]
        wsp = jnp.concatenate([ws[0::2], ws[1::2]], axis=-1).astype(BF16)
        bs_tab = jnp.repeat(gmlp_b_s[l].T, D_GMLP // N_GMLP_GROUPS, axis=1)
        prm = dict(
            rope=_rope_table(max(x_prompt.shape[1], x_sample.shape[1])),
            g_pre=norm_mix_pre[l][None, :],
            w_in=w_in[l].astype(BF16),
            ln_g=gmlp_ln_g[l][None, :], ln_b=gmlp_ln_b[l][None, :],
            sink=attn_sink[l],
            wsp=wsp, bs_tab=bs_tab,
            g_attn=out_norm_attn[l][None, :], g_gmlp=out_norm_gmlp[l][None, :],
            w_o=w_o[l].astype(BF16),
            g_mix_post=norm_mix_post[l][None, :],
            g_ffn_pre=norm_ffn_pre[l][None, :],
            w_ffn_in=w_ffn_in[l].astype(BF16),
            cg=_chunked_conv(conv_w[l][:, :D_FF], conv_b[l][:D_FF]),
            cu=_chunked_conv(conv_w[l][:, D_FF:], conv_b[l][D_FF:]),
            wd=w_ffn_out[l].astype(BF16),
            g_ffn_post=norm_ffn_post[l][None, :],
        )
        x_prompt = _layer(x_prompt, prm)
        x_sample = _layer(x_sample, prm)
    return (x_prompt, x_sample)
```

```python
import numpy as np
import jax
import jax.numpy as jnp
from jax import lax
from jax.experimental import pallas as pl
from jax.experimental.pallas import tpu as pltpu

D_MODEL = 1024
D_ATTN = 512
D_GMLP = 512
HEAD_DIM = 64
N_Q_HEADS = 8
N_KV_HEADS = 2
ROT_DIM = 16
ROPE_THETA = 500000.0
WINDOW = 128
BLOCK = 128
N_GMLP_GROUPS = 8
CHUNK = 128
D_FF = 2816
EPS = 1e-6
D_Q = N_Q_HEADS * HEAD_DIM
D_KV = N_KV_HEADS * HEAD_DIM
D_IN_PROJ = D_Q + 2 * D_KV + 2 * D_GMLP

LANES = 128
F32_ROWS = 8
BF16_ROWS = 16
FF_CHUNK = 256
N_FF_CHUNKS = D_FF // FF_CHUNK
FF_GROUP = 2
VMEM_LIMIT = 56 * 1024 * 1024

PROJ_TILE = 1024
MIX_TILE = 1024
FFN_TILE = 1024

F32 = jnp.float32
BF16 = jnp.bfloat16
SQRT_HALF = float(np.sqrt(0.5))
LOG2E = float(np.log2(np.e))
NEG = float(np.finfo(np.float32).min)


def _rms(x, g):
    ms = jnp.mean(x * x, axis=-1, keepdims=True)
    return x * lax.rsqrt(ms + EPS) * g


def _gelu(x):
    return 0.5 * x * (1.0 + lax.erf(x * SQRT_HALF))


def _proj_kernel(x_ref, g_ref, w_ref, rope_ref, lng_ref, lnb_ref,
                 q_ref, k4_ref, v4_ref, u_ref, vg2_ref, hbuf):
    hbuf[...] = _rms(x_ref[...], g_ref[...]).astype(BF16)

    def project(lo, hi):
        return jnp.dot(hbuf[...], w_ref[:, lo:hi], preferred_element_type=F32)

    cos = rope_ref[0]
    sin_up = rope_ref[1]
    sin_dn = rope_ref[2]
    lane = lax.broadcasted_iota(jnp.int32, (1, LANES), 1)
    first_head = lane < HEAD_DIM

    def rope(t):
        return (t * cos + pltpu.roll(t, ROT_DIM // 2, 1) * sin_up
                + pltpu.roll(t, LANES - ROT_DIM // 2, 1) * sin_dn)

    def spread(t, out_ref):
        sw = pltpu.roll(t, HEAD_DIM, 1)
        zero = jnp.zeros_like(t)
        parts = (jnp.where(first_head, t, zero), jnp.where(first_head, zero, sw),
                 jnp.where(first_head, sw, zero), jnp.where(first_head, zero, t))
        for i, part in enumerate(parts):
            out_ref[:, i * LANES:(i + 1) * LANES] = part.astype(BF16)

    def finish_gate(p):
        gv = _gelu(p)
        mu = jnp.mean(gv, axis=-1, keepdims=True)
        cen = gv - mu
        var = jnp.mean(cen * cen, axis=-1, keepdims=True)
        vn = cen * lax.rsqrt(var + EPS) * lng_ref[...] + lnb_ref[...]
        for i in range(D_GMLP // LANES):
            t = vn[:, i * LANES:(i + 1) * LANES]
            zero = jnp.zeros_like(t)
            vg2_ref[:, i * LANES:(i + 1) * LANES] = jnp.where(first_head, t, zero).astype(BF16)
            vg2_ref[:, D_GMLP + i * LANES:D_GMLP + (i + 1) * LANES] = (
                jnp.where(first_head, zero, t).astype(BF16))

    def finish_kv(p):
        spread(rope(p[:, 0:D_KV]), k4_ref)
        spread(p[:, D_KV:], v4_ref)

    def finish_q(p):
        scale = HEAD_DIM ** -0.5 * LOG2E
        for i in range(D_Q // LANES):
            sl = slice(i * LANES, (i + 1) * LANES)
            q_ref[:, sl] = (rope(p[:, sl]) * scale).astype(BF16)

    o3 = D_Q + 2 * D_KV
    p_q = project(0, D_Q)
    p_kv = project(D_Q, o3)
    p_gate = project(o3 + D_GMLP, D_IN_PROJ)
    finish_q(p_q)
    finish_kv(p_kv)
    p_u = project(o3, o3 + D_GMLP)
    finish_gate(p_gate)
    u_ref[...] = _gelu(p_u).astype(BF16)


def _const_spec(shape):
    nd = len(shape)
    return pl.BlockSpec(shape, lambda b, j: (0,) * nd, pipeline_mode=pl.Buffered(1))


def _proj_call(x, g_pre, w_in, rope_tab, ln_g, ln_b):
    B, L, D = x.shape
    T = PROJ_TILE
    assert D == D_MODEL and L % T == 0 and rope_tab.shape[1] >= L, (x.shape, rope_tab.shape)
    assert w_in.shape == (D, D_IN_PROJ) and w_in.dtype == BF16, (w_in.shape, w_in.dtype)
    row = lambda width: pl.BlockSpec((None, T, width), lambda b, j: (b, j, 0))
    out_widths = (D_Q, 4 * LANES, 4 * LANES, D_GMLP, 2 * D_GMLP)
    return pl.pallas_call(
        _proj_kernel,
        grid=(B, L // T),
        in_specs=[
            row(D),
            _const_spec((1, D)),
            _const_spec((D, D_IN_PROJ)),
            pl.BlockSpec((3, T, LANES), lambda b, j: (0, j, 0)),
            _const_spec((1, D_GMLP)),
            _const_spec((1, D_GMLP)),
        ],
        out_specs=[row(w) for w in out_widths],
        out_shape=[jax.ShapeDtypeStruct((B, L, w), BF16) for w in out_widths],
        scratch_shapes=[pltpu.VMEM((T, D), BF16)],
        compiler_params=pltpu.CompilerParams(
            dimension_semantics=("arbitrary", "arbitrary"),
            vmem_limit_bytes=VMEM_LIMIT),
        name="proj",
    )(x, g_pre, w_in, rope_tab, ln_g, ln_b)


def _mix_kernel(sink_ref, x_ref, q_ref, k_ref, kp_ref, kn_ref,
                v_ref, vp_ref, vn_ref, u_ref, vg2_ref, wsp_ref, bs_ref,
                ga_ref, gg_ref, wo_ref, gpost_ref, o_ref, kbuf, vbuf, mixed):
    T = x_ref.shape[0]
    nb = T // BLOCK
    j = pl.program_id(1)

    kbuf[0:BLOCK] = kp_ref[...]
    kbuf[BLOCK:BLOCK + T] = k_ref[...]
    kbuf[BLOCK + T:] = kn_ref[...]
    vbuf[0:BLOCK] = vp_ref[...]
    vbuf[BLOCK:BLOCK + T] = v_ref[...]
    vbuf[BLOCK + T:] = vn_ref[...]

    qi = lax.broadcasted_iota(jnp.int32, (2 * BLOCK, BLOCK), 0) % BLOCK
    ki = lax.broadcasted_iota(jnp.int32, (2 * BLOCK, BLOCK), 1)
    neg = jnp.full((2 * BLOCK, BLOCK), NEG, F32)
    zero = jnp.zeros((2 * BLOCK, BLOCK), F32)
    bias_prev = jnp.where(ki >= qi, zero, neg)
    bias_next = jnp.where(ki <= qi, zero, neg)
    top_rows = lax.broadcasted_iota(jnp.int32, (2 * BLOCK, 1), 0) < BLOCK
    nt = (((1,), (1,)), ((), ()))

    def biases(n):
        bias_p, bias_n = bias_prev, bias_next
        if n == 0:
            bias_p = jnp.where(j > 0, bias_prev, neg)
        if n == nb - 1:
            bias_n = jnp.where(j < pl.num_programs(1) - 1, bias_next, neg)
        return bias_p, bias_n

    def scores(n, g, half):
        rows = slice(n * BLOCK, (n + 1) * BLOCK)
        keys = slice(n * BLOCK, (n + 3) * BLOCK)
        c0 = 2 * g * LANES
        lhs = jnp.concatenate(
            [q_ref[rows, c0:c0 + LANES], q_ref[rows, c0 + LANES:c0 + 2 * LANES]], axis=0)
        cs = slice((2 * g + half) * LANES, (2 * g + half + 1) * LANES)
        return lax.dot_general(lhs, kbuf[keys, cs], nt, preferred_element_type=F32)

    def softmax(s, g, half, bias_p, bias_n):
        s0 = s[:, 0:BLOCK] + bias_p
        s1 = s[:, BLOCK:2 * BLOCK]
        s2 = s[:, 2 * BLOCK:] + bias_n
        sink = jnp.where(top_rows, sink_ref[4 * g + half], sink_ref[4 * g + 2 + half]) * LOG2E
        m = jnp.max(jnp.maximum(jnp.maximum(s0, s1), s2), axis=-1, keepdims=True)
        m = jnp.maximum(m, sink)
        e0 = jnp.exp2(s0 - m)
        e1 = jnp.exp2(s1 - m)
        e2 = jnp.exp2(s2 - m)
        den = jnp.sum(e0 + e1 + e2, axis=-1, keepdims=True) + jnp.exp2(sink - m)
        return jnp.concatenate([e0, e1, e2], axis=-1).astype(BF16), 1.0 / den

    def weighted_values(e, rden, n, g, half):
        keys = slice(n * BLOCK, (n + 3) * BLOCK)
        cs = slice((2 * g + half) * LANES, (2 * g + half + 1) * LANES)
        return jnp.dot(e, vbuf[keys, cs], preferred_element_type=F32) * rden

    sg_of_block = {}

    def spatial_gate(n0):
        halves = ([], [])
        for pair in range(N_GMLP_GROUPS // 2):
            ps = slice(pair * LANES, (pair + 1) * LANES)
            ps_hi = slice(D_GMLP + pair * LANES, D_GMLP + (pair + 1) * LANES)
            rhs = jnp.concatenate(
                [jnp.concatenate([vg2_ref[n * BLOCK:(n + 1) * BLOCK, ps],
                                  vg2_ref[n * BLOCK:(n + 1) * BLOCK, ps_hi]], axis=0)
                 for n in (n0, n0 + 1)], axis=1)
            both = jnp.dot(wsp_ref[pair], rhs, preferred_element_type=F32)
            halves[0].append(both[:, 0:LANES])
            halves[1].append(both[:, LANES:])
        sg_of_block[n0] = jnp.concatenate(halves[0], axis=-1)
        sg_of_block[n0 + 1] = jnp.concatenate(halves[1], axis=-1)

    def finish_block(n, outs):
        rows = slice(n * BLOCK, (n + 1) * BLOCK)
        a0 = outs[(0, 0)] + outs[(0, 1)]
        a1 = outs[(1, 0)] + outs[(1, 1)]
        attn = jnp.concatenate([a0[0:BLOCK], a0[BLOCK:], a1[0:BLOCK], a1[BLOCK:]], axis=-1)
        mixed[rows, 0:D_ATTN] = _rms(attn, ga_ref[...]).astype(BF16)
        if n % 2 == 0:
            spatial_gate(n)
        sg = sg_of_block.pop(n) + bs_ref[...]
        gm = u_ref[rows, :].astype(F32) * sg
        mixed[rows, D_ATTN:] = _rms(gm, gg_ref[...]).astype(BF16)

    chains = [(n, g, half) for n in range(nb) for g in range(N_KV_HEADS) for half in range(2)]
    last_of_block = (N_KV_HEADS - 1, 1)
    outs = {}

    def retire(e, rden, n, g, half):
        outs[(g, half)] = weighted_values(e, rden, n, g, half)
        if (g, half) == last_of_block:
            finish_block(n, outs)
            outs.clear()

    next_scores = scores(*chains[0])
    waiting = None
    bias_pn = biases(0)
    for i, (n, g, half) in enumerate(chains):
        s = next_scores
        if i + 1 < len(chains):
            next_scores = scores(*chains[i + 1])
        e, rden = softmax(s, g, half, *bias_pn)
        if waiting is not None:
            retire(*waiting)
        waiting = (e, rden, n, g, half)
        if (g, half) == last_of_block and n + 1 < nb:
            bias_pn = biases(n + 1)
    retire(*waiting)

    for h in range(2):
        half_rows = slice(h * (T // 2), (h + 1) * (T // 2))
        mm = jnp.dot(mixed[half_rows, :], wo_ref[...], preferred_element_type=F32)
        o_ref[half_rows, :] = x_ref[half_rows, :] + _rms(mm, gpost_ref[...])


def _mix_call(x, q, k4, v4, u, vg2, sink, wsp, bs_tab, g_attn, g_gmlp, w_o, g_post):
    B, L, D = x.shape
    T = MIX_TILE
    nb = T // BLOCK
    n_seq_blocks = L // BLOCK
    assert D == D_MODEL and L % T == 0 and nb % 2 == 0, (x.shape, T)
    assert w_o.shape == (D_ATTN + D_GMLP, D) and w_o.dtype == BF16, (w_o.shape, w_o.dtype)
    row = lambda width: pl.BlockSpec((None, T, width), lambda b, j: (b, j, 0))
    prev = pl.BlockSpec((None, BLOCK, 4 * LANES),
                        lambda b, j: (b, jnp.maximum(j * nb - 1, 0), 0))
    nxt = pl.BlockSpec((None, BLOCK, 4 * LANES),
                       lambda b, j: (b, jnp.minimum((j + 1) * nb, n_seq_blocks - 1), 0))
    return pl.pallas_call(
        _mix_kernel,
        grid=(B, L // T),
        in_specs=[
            pl.BlockSpec(memory_space=pltpu.SMEM),
            row(D), row(D_Q),
            row(4 * LANES), prev, nxt,
            row(4 * LANES), prev, nxt,
            row(D_GMLP), row(2 * D_GMLP),
            _const_spec((N_GMLP_GROUPS // 2, CHUNK, 2 * CHUNK)),
            _const_spec((CHUNK, D_GMLP)),
            _const_spec((1, D_ATTN)),
            _const_spec((1, D_GMLP)),
            _const_spec((D_ATTN + D_GMLP, D)),
            _const_spec((1, D)),
        ],
        out_specs=row(D),
        out_shape=jax.ShapeDtypeStruct((B, L, D), F32),
        scratch_shapes=[
            pltpu.VMEM((T + 2 * BLOCK, 4 * LANES), BF16),
            pltpu.VMEM((T + 2 * BLOCK, 4 * LANES), BF16),
            pltpu.VMEM((T, D_ATTN + D_GMLP), BF16),
        ],
        compiler_params=pltpu.CompilerParams(
            dimension_semantics=("arbitrary", "arbitrary"),
            vmem_limit_bytes=VMEM_LIMIT),
        name="mix",
    )(sink, x, q, k4, k4, k4, v4, v4, v4, u, vg2, wsp, bs_tab, g_attn, g_gmlp, w_o, g_post)


def _ffn_kernel(x_ref, xp_ref, xn_ref, gpre_ref, w_ref, cg_ref, cu_ref,
                wd_ref, gpost_ref, o_ref, hbuf, fbuf):
    T = x_ref.shape[0]
    H = BF16_ROWS
    j = pl.program_id(1)
    g = gpre_ref[...]

    hbuf[0:T] = _rms(x_ref[...], g).astype(BF16)
    hp = _rms(xp_ref[H - 1:H, :], g)
    row_before = jnp.where(j > 0, hp, jnp.zeros_like(hp))
    hn = _rms(xn_ref[0:1, :], g)
    row_after = jnp.where(j < pl.num_programs(1) - 1, hn, jnp.zeros_like(hn))
    rid = lax.broadcasted_iota(jnp.int32, (H, 1), 0)
    hbuf[T:] = jnp.where(rid == 0, row_before, jnp.where(rid == 1, row_after, 0.0)).astype(BF16)

    sub = lax.broadcasted_iota(jnp.int32, (F32_ROWS, 1), 0)

    def conv(t, c):
        main = t[0:T]
        before = pltpu.roll(main, 1, 0)
        after = pltpu.roll(main, T - 1, 0)
        before = jnp.concatenate(
            [jnp.where(sub == 0, t[T:T + 1], before[0:F32_ROWS]), before[F32_ROWS:]], axis=0)
        after = jnp.concatenate(
            [after[0:T - F32_ROWS],
             jnp.where(sub == F32_ROWS - 1, t[T + 1:T + 2], after[T - F32_ROWS:])], axis=0)
        return before * c[0:1] + main * c[1:2] + after * c[2:3] + c[3:4]

    for c0 in range(0, N_FF_CHUNKS, FF_GROUP):
        c1 = min(c0 + FF_GROUP, N_FF_CHUNKS)
        cols = slice(c0 * FF_CHUNK, c1 * FF_CHUNK)
        up_cols = slice(D_FF + c0 * FF_CHUNK, D_FF + c1 * FF_CHUNK)
        conv_g = jnp.concatenate([cg_ref[c] for c in range(c0, c1)], axis=1)
        conv_u = jnp.concatenate([cu_ref[c] for c in range(c0, c1)], axis=1)
        a = conv(jnp.dot(hbuf[...], w_ref[:, cols], preferred_element_type=F32), conv_g)
        b = conv(jnp.dot(hbuf[...], w_ref[:, up_cols], preferred_element_type=F32), conv_u)
        fbuf[:, cols] = (a * jax.nn.sigmoid(a) * b).astype(BF16)
    acc = jnp.dot(fbuf[...], wd_ref[...], preferred_element_type=F32)
    o_ref[...] = x_ref[...] + _rms(acc, gpost_ref[...])


def _ffn_call(x, g_pre, w_in, cg, cu, wd, g_post):
    B, L, D = x.shape
    T = FFN_TILE
    H = BF16_ROWS
    per = T // H
    n_halo = L // H
    assert D == D_MODEL and L % T == 0 and T % H == 0, (x.shape, T)
    assert w_in.shape == (D, 2 * D_FF) and wd.shape == (D_FF, D), (w_in.shape, wd.shape)
    assert w_in.dtype == BF16 and wd.dtype == BF16, (w_in.dtype, wd.dtype)
    row = pl.BlockSpec((None, T, D), lambda b, j: (b, j, 0))
    prev = pl.BlockSpec((None, H, D), lambda b, j: (b, jnp.maximum(j * per - 1, 0), 0))
    nxt = pl.BlockSpec((None, H, D), lambda b, j: (b, jnp.minimum((j + 1) * per, n_halo - 1), 0))
    return pl.pallas_call(
        _ffn_kernel,
        grid=(B, L // T),
        in_specs=[
            row, prev, nxt,
            _const_spec((1, D)),
            _const_spec((D, 2 * D_FF)),
            _const_spec((N_FF_CHUNKS, 4, FF_CHUNK)),
            _const_spec((N_FF_CHUNKS, 4, FF_CHUNK)),
            _const_spec((D_FF, D)),
            _const_spec((1, D)),
        ],
        out_specs=row,
        out_shape=jax.ShapeDtypeStruct((B, L, D), F32),
        scratch_shapes=[pltpu.VMEM((T + H, D), BF16), pltpu.VMEM((T, D_FF), BF16)],
        compiler_params=pltpu.CompilerParams(
            dimension_semantics=("arbitrary", "arbitrary"),
            vmem_limit_bytes=VMEM_LIMIT),
        name="ffn",
    )(x, x, x, g_pre, w_in, cg, cu, wd, g_post)


def _rope_table(L):
    half = ROT_DIM // 2
    inv_freq = ROPE_THETA ** (-jnp.arange(0, ROT_DIM, 2, dtype=F32) / ROT_DIM)
    ang = jnp.arange(L, dtype=F32)[:, None] * inv_freq[None, :]
    lane = np.arange(LANES) % HEAD_DIM
    idx = lane % half
    cos = jnp.cos(ang)[:, idx]
    sin = jnp.sin(ang)[:, idx]
    cos_t = jnp.where(lane < ROT_DIM, cos, 1.0)
    sin_up = jnp.where((lane >= half) & (lane < ROT_DIM), sin, 0.0)
    sin_dn = jnp.where(lane < half, -sin, 0.0)
    return jnp.stack([cos_t, sin_up, sin_dn]).astype(F32)


def _chunked_conv(w, b):
    t = jnp.concatenate([w, b[None, :]], axis=0)
    return t.reshape(4, N_FF_CHUNKS, FF_CHUNK).transpose(1, 0, 2)


def _layer(x, prm):
    q, k4, v4, u, vg2 = _proj_call(x, prm["g_pre"], prm["w_in"], prm["rope"],
                                   prm["ln_g"], prm["ln_b"])
    x = _mix_call(x, q, k4, v4, u, vg2, prm["sink"], prm["wsp"], prm["bs_tab"],
                  prm["g_attn"], prm["g_gmlp"], prm["w_o"], prm["g_mix_post"])
    return _ffn_call(x, prm["g_ffn_pre"], prm["w_ffn_in"], prm["cg"], prm["cu"],
                     prm["wd"], prm["g_ffn_post"])


def kernel(x_prompt, x_sample, norm_mix_pre, w_in, attn_sink, gmlp_ln_g, gmlp_ln_b, gmlp_w_s, gmlp_b_s, out_norm_attn, out_norm_gmlp, w_o, norm_mix_post, norm_ffn_pre, w_ffn_in, conv_w, conv_b, w_ffn_out, norm_ffn_post):
    depth = w_in.shape[0]
    for l in range(depth):
        ws = gmlp_w_s[l]
        wsp = jnp.concatenate([ws[0::2], ws[1::2]], axis=-1).astype(BF16)
        bs_tab = jnp.repeat(gmlp_b_s[l].T, D_GMLP // N_GMLP_GROUPS, axis=1)
        prm = dict(
            rope=_rope_table(max(x_prompt.shape[1], x_sample.shape[1])),
            g_pre=norm_mix_pre[l][None, :],
            w_in=w_in[l].astype(BF16),
            ln_g=gmlp_ln_g[l][None, :], ln_b=gmlp_ln_b[l][None, :],
            sink=attn_sink[l],
            wsp=wsp, bs_tab=bs_tab,
            g_attn=out_norm_attn[l][None, :], g_gmlp=out_norm_gmlp[l][None, :],
            w_o=w_o[l].astype(BF16),
            g_mix_post=norm_mix_post[l][None, :],
            g_ffn_pre=norm_ffn_pre[l][None, :],
            w_ffn_in=w_ffn_in[l].astype(BF16),
            cg=_chunked_conv(conv_w[l][:, :D_FF], conv_b[l][:D_FF]),
            cu=_chunked_conv(conv_w[l][:, D_FF:], conv_b[l][D_FF:]),
            wd=w_ffn_out[l].astype(BF16),
            g_ffn_post=norm_ffn_post[l][None, :],
        )
        x_prompt = _layer(x_prompt, prm)
        x_sample = _layer(x_sample, prm)
    return (x_prompt, x_sample)
```
